```python
import math
import jax, jax.numpy as jnp
from jax import lax
import numpy as np

D_MODEL = 2048
BATCH = 4
SEQ = 2048
DEPTH = 2
DEC_BATCH = 8
DEC_SEQ = 4
PAST_LEN = 16384
PAGE_SIZE = 128

N_MIXERS = 2
N_RET_LAYERS = (DEPTH + 1) // 2
N_DSA_LAYERS = DEPTH // 2

RET_HEADS = 8
RET_DK = D_MODEL // RET_HEADS
RET_DV = 2 * D_MODEL // RET_HEADS
RET_CHUNK = 128
ROPE_BASE = 10000.0
RET_IN = 2 * RET_HEADS * RET_DK + 2 * RET_HEADS * RET_DV

ATT_HEADS = 16
ATT_DH = D_MODEL // ATT_HEADS
KV_HEADS = 4
ATT_GROUP = ATT_HEADS // KV_HEADS
IDX_HEADS = 16
IDX_DH = 128
TOPK_MAX = 256
ATT_QBLK = 64
T5_BUCKETS = 32
T5_MAX_DIST = 128
Q_W = ATT_HEADS * ATT_DH
KV_W = KV_HEADS * ATT_DH
IQ_W = IDX_HEADS * IDX_DH
DSA_IN = Q_W + 2 * KV_W + IQ_W + IDX_DH + IDX_HEADS

N_EXPERTS = 32
TOP_K = 4
D_EXPERT = D_MODEL
SWIGLU_LIMIT = 7.0
SWIGLU_ALPHA = 1.702
MOE_BLK = 128

DEEPNORM_ALPHA = (2.0 * DEPTH) ** 0.25
DEEPNORM_BETA = (8.0 * DEPTH) ** -0.25
LN_EPS = 1e-5

kernel_name = "retention_dsa_moe_hybrid_step"


def _layer_norm(x, g, b):
    xf = x.astype(jnp.float32)
    mu = jnp.mean(xf, axis=-1, keepdims=True)
    var = jnp.mean(jnp.square(xf - mu), axis=-1, keepdims=True)
    y = (xf - mu) * lax.rsqrt(var + LN_EPS)
    return (y * g.astype(jnp.float32) + b.astype(jnp.float32)).astype(x.dtype)


def _rotary(x, pos):
    half = x.shape[-1] // 2
    inv = ROPE_BASE ** (-jnp.arange(half, dtype=jnp.float32) / half)
    ang = pos.astype(jnp.float32)[:, None] * inv[None, :]
    cos = jnp.cos(ang)[:, None, :]
    sin = jnp.sin(ang)[:, None, :]
    x1, x2 = x[..., :half], x[..., half:]
    return jnp.concatenate([x1 * cos - x2 * sin, x1 * sin + x2 * cos], axis=-1)


def _ret_log_decay():
    return jnp.log1p(-jnp.exp2(-5.0 - jnp.arange(RET_HEADS, dtype=jnp.float32)))


def _retention_chunk(state, qkv, log_g):
    q, k, v = qkv
    C = q.shape[1]
    pos = jnp.arange(C, dtype=jnp.float32)
    rel = pos[:, None] - pos[None, :]
    intra = jnp.where(rel[None] >= 0, jnp.exp(log_g[:, None, None] * jnp.maximum(rel, 0.0)[None]), 0.0)
    scores = jnp.einsum('bihd,bjhd->bhij', q, k) * intra[None]
    o = jnp.einsum('bhij,bjhe->bihe', scores, v)
    q_decay = jnp.exp((pos[:, None] + 1.0) * log_g[None, :])
    o = o + jnp.einsum('bihd,bhde->bihe', q, state) * q_decay[None, :, :, None]
    k_decay = jnp.exp((C - 1.0 - pos)[:, None] * log_g[None, :])
    state = state * jnp.exp(C * log_g)[None, :, None, None] + jnp.einsum(
        'bjhd,bjhe->bhde', k * k_decay[None, :, :, None], v)
    return state, o


def retention_mixer(x, state0, pos0, w_in, gn_g, gn_b, w_o):
    B, T, _ = x.shape
    h = x @ w_in
    s1 = RET_HEADS * RET_DK
    s2 = 2 * s1
    s3 = s2 + RET_HEADS * RET_DV
    q, k, v, g = h[..., :s1], h[..., s1:s2], h[..., s2:s3], h[..., s3:]
    pos = pos0 + jnp.arange(T, dtype=jnp.int32)
    q = _rotary(q.reshape(B, T, RET_HEADS, RET_DK).astype(jnp.float32), pos)
    k = _rotary(k.reshape(B, T, RET_HEADS, RET_DK).astype(jnp.float32), pos) * (RET_DK ** -0.5)
    v = v.reshape(B, T, RET_HEADS, RET_DV).astype(jnp.float32)
    C = RET_CHUNK if T % RET_CHUNK == 0 else T
    n = T // C

    def to_chunks(a):
        return jnp.moveaxis(a.reshape(B, n, C, *a.shape[2:]), 1, 0)

    log_g = _ret_log_decay()
    state, o = lax.scan(lambda s, c: _retention_chunk(s, c, log_g), state0.astype(jnp.float32),
                        (to_chunks(q), to_chunks(k), to_chunks(v)))
    o = jnp.moveaxis(o, 0, 1).reshape(B, T, RET_HEADS, RET_DV)
    o = _layer_norm(o, gn_g.reshape(RET_HEADS, RET_DV), gn_b.reshape(RET_HEADS, RET_DV))
    o = o.reshape(B, T, RET_HEADS * RET_DV).astype(x.dtype) * jax.nn.silu(g)
    return o @ w_o, state.astype(x.dtype)


def _dsa_project(x, w_in, kidx_g, kidx_b):
    B, T, _ = x.shape
    h = x @ w_in
    o1 = Q_W
    o2 = o1 + KV_W
    o3 = o2 + KV_W
    o4 = o3 + IQ_W
    o5 = o4 + IDX_DH
    q = h[..., :o1].reshape(B, T, ATT_HEADS, ATT_DH)
    k = h[..., o1:o2].reshape(B, T, KV_HEADS, ATT_DH)
    v = h[..., o2:o3].reshape(B, T, KV_HEADS, ATT_DH)
    iq = h[..., o3:o4].reshape(B, T, IDX_HEADS, IDX_DH)
    ik = _layer_norm(h[..., o4:o5], kidx_g, kidx_b)
    iw = h[..., o5:] * (IDX_HEADS ** -0.5 * IDX_DH ** -0.5)
    return q, k, v, iq, ik, iw


def _indexer_scores(iq, iw, ik):
    dots = jnp.einsum('bthd,bsd->bths', iq, ik).astype(jnp.float32)
    return jnp.einsum('bths,bth->bts', jax.nn.relu(dots), iw.astype(jnp.float32))


def _t5_bucket(dist):
    max_exact = T5_BUCKETS // 2
    is_small = dist < max_exact
    d = jnp.maximum(dist, 1).astype(jnp.float32)
    large = max_exact + (jnp.log(d / max_exact) / math.log(T5_MAX_DIST / max_exact)
                         * (T5_BUCKETS - max_exact)).astype(jnp.int32)
    large = jnp.minimum(large, T5_BUCKETS - 1)
    return jnp.where(is_small, dist, large)


def _sparse_attend(q, k_sel, v_sel, q_pos, k_pos, valid, t5_bias):
    B, T = q.shape[:2]
    K = k_sel.shape[2]
    qg = q.reshape(B, T, KV_HEADS, ATT_GROUP, ATT_DH)
    s = jnp.einsum('btngd,btknd->btngk', qg, k_sel).astype(jnp.float32) * (ATT_DH ** -0.5)
    bucket = _t5_bucket(jnp.maximum(q_pos[None, :, None] - k_pos, 0))
    bias = t5_bias[bucket].astype(jnp.float32).reshape(B, T, K, KV_HEADS, ATT_GROUP)
    s = s + jnp.transpose(bias, (0, 1, 3, 4, 2))
    s = jnp.where(valid[:, :, None, None, :], s, -jnp.inf)
    p = jax.nn.softmax(s, axis=-1).astype(v_sel.dtype)
    o = jnp.einsum('btngk,btknd->btngd', p, v_sel)
    return o.reshape(B, T, Q_W)


def dsa_prompt(x, w_in, kidx_g, kidx_b, w_o, t5_bias):
    B, T, _ = x.shape
    q, k, v, iq, ik, iw = _dsa_project(x, w_in, kidx_g, kidx_b)
    topk = min(TOPK_MAX, T // 4)
    qb = min(ATT_QBLK, T)
    nb = T // qb
    key_pos = jnp.arange(T, dtype=jnp.int32)
    bidx = jnp.arange(B)[:, None, None]

    def blocks(a):
        return jnp.moveaxis(a.reshape(B, nb, qb, *a.shape[2:]), 1, 0)

    def attend_block(args):
        q_b, iq_b, iw_b, qpos = args
        sc = _indexer_scores(iq_b, iw_b, ik)
        sc = jnp.where(key_pos[None, None, :] <= qpos[None, :, None], sc, -jnp.inf)
        _, idx = lax.top_k(sc, topk)
        valid = idx <= qpos[None, :, None]
        return _sparse_attend(q_b, k[bidx, idx], v[bidx, idx], qpos, idx, valid, t5_bias)

    o = lax.map(attend_block, (blocks(q), blocks(iq), blocks(iw), key_pos.reshape(nb, qb)))
    o = jnp.moveaxis(o, 0, 1).reshape(B, T, Q_W)
    return o @ w_o, k, v, ik


def dsa_sample(x, cache_k, cache_v, cache_kidx, page_table, w_in, kidx_g, kidx_b, w_o, t5_bias):
    B, T, _ = x.shape
    n_pages = PAST_LEN // PAGE_SIZE
    L = PAST_LEN + T
    topk = min(TOPK_MAX, L // 4)
    q, k, v, iq, ik, iw = _dsa_project(x, w_in, kidx_g, kidx_b)
    q_pos = PAST_LEN + jnp.arange(T, dtype=jnp.int32)
    ik_past = cache_kidx[page_table].reshape(B, n_pages * PAGE_SIZE, IDX_DH)
    ik_all = jnp.concatenate([ik_past, ik.astype(ik_past.dtype)], axis=1)
    sc = _indexer_scores(iq, iw, ik_all)
    key_pos = jnp.arange(L, dtype=jnp.int32)
    sc = jnp.where(key_pos[None, None, :] <= q_pos[None, :, None], sc, -jnp.inf)
    _, idx = lax.top_k(sc, topk)
    valid = idx <= q_pos[None, :, None]
    bidx = jnp.arange(B)[:, None, None]
    is_past = idx < PAST_LEN
    pidx = jnp.minimum(idx, PAST_LEN - 1)
    phys = page_table[bidx, pidx // PAGE_SIZE]
    off = pidx % PAGE_SIZE
    nidx = jnp.clip(idx - PAST_LEN, 0, T - 1)
    k_sel = jnp.where(is_past[..., None, None], cache_k[phys, off], k[bidx, nidx].astype(cache_k.dtype))
    v_sel = jnp.where(is_past[..., None, None], cache_v[phys, off], v[bidx, nidx].astype(cache_v.dtype))
    o = _sparse_attend(q, k_sel.astype(q.dtype), v_sel.astype(q.dtype), q_pos, idx, valid, t5_bias)
    return o @ w_o, k, v, ik


def moe_ffn(x, w_router, b_router, w_gu, b_gu, w_down, b_down):
    shp = x.shape
    xt = x.reshape(-1, shp[-1])
    T = xt.shape[0]
    logits = (xt @ w_router).astype(jnp.float32) + b_router.astype(jnp.float32)
    top_v, top_e = lax.top_k(logits, TOP_K)
    gates = jax.nn.softmax(top_v, axis=-1)
    n_pairs = T * TOP_K
    flat_e = top_e.reshape(-1)
    order = jnp.argsort(flat_e)
    sorted_e = flat_e[order]
    counts = jnp.bincount(flat_e, length=N_EXPERTS)
    padded = ((counts + MOE_BLK - 1) // MOE_BLK) * MOE_BLK
    start = jnp.cumsum(counts) - counts
    pend = jnp.cumsum(padded)
    pstart = pend - padded
    dest = pstart[sorted_e] + (jnp.arange(n_pairs) - start[sorted_e])
    n_blocks = (n_pairs + MOE_BLK - 1) // MOE_BLK + N_EXPERTS
    n_rows = n_blocks * MOE_BLK
    row_tok = jnp.full((n_rows,), T, jnp.int32).at[dest].set((order // TOP_K).astype(jnp.int32))
    x_pad = jnp.concatenate([xt, jnp.zeros((1, xt.shape[1]), xt.dtype)], axis=0)
    rows = x_pad[row_tok].reshape(n_blocks, MOE_BLK, xt.shape[1])
    block_e = jnp.minimum(jnp.searchsorted(pend, jnp.arange(n_blocks) * MOE_BLK, side='right'), N_EXPERTS - 1)

    def expert_block(args):
        xb, e = args
        gu = xb @ w_gu[e] + b_gu[e]
        gate = jnp.minimum(gu[:, :D_EXPERT], SWIGLU_LIMIT)
        up = jnp.clip(gu[:, D_EXPERT:], -SWIGLU_LIMIT, SWIGLU_LIMIT)
        hid = (up + 1.0) * (gate * jax.nn.sigmoid(gate * SWIGLU_ALPHA))
        return hid @ w_down[e] + b_down[e]

    out_rows = lax.map(expert_block, (rows, block_e)).reshape(n_rows, xt.shape[1])
    pair_out = jnp.zeros((n_pairs, xt.shape[1]), out_rows.dtype).at[order].set(out_rows[dest])
    y = jnp.einsum('tkd,tk->td', pair_out.reshape(T, TOP_K, -1), gates.astype(out_rows.dtype))
    return y.reshape(shp).astype(x.dtype)


def setup_inputs(seed: int = 0) -> dict:
    key = jax.random.key(seed)
    ks = jax.random.split(key, 32)
    f32 = jnp.float32
    n_pages = PAST_LEN // PAGE_SIZE
    n_pool = (DEC_BATCH * n_pages * 5) // 4

    def nrm(k, shape, s):
        return jax.random.normal(k, shape, f32) * s

    ret_col = jnp.concatenate([jnp.ones((2 * RET_HEADS * RET_DK,), f32),
                               jnp.full((RET_HEADS * RET_DV,), DEEPNORM_BETA, f32),
                               jnp.ones((RET_HEADS * RET_DV,), f32)])
    dsa_col = jnp.concatenate([jnp.ones((Q_W + KV_W,), f32),
                               jnp.full((KV_W,), DEEPNORM_BETA, f32),
                               jnp.ones((IQ_W + IDX_DH + IDX_HEADS,), f32)])
    page_table = jax.random.permutation(ks[6], n_pool)[:DEC_BATCH * n_pages].reshape(DEC_BATCH, n_pages).astype(jnp.int32)
    return {
        "x_prompt": nrm(ks[0], (BATCH, SEQ, D_MODEL), 1.0),
        "x_sample": nrm(ks[1], (DEC_BATCH, DEC_SEQ, D_MODEL), 1.0),
        "state_ret": nrm(ks[2], (N_RET_LAYERS, DEC_BATCH, RET_HEADS, RET_DK, RET_DV), 0.05),
        "cache_k": nrm(ks[3], (N_DSA_LAYERS, n_pool, PAGE_SIZE, KV_HEADS, ATT_DH), 1.0),
        "cache_v": nrm(ks[4], (N_DSA_LAYERS, n_pool, PAGE_SIZE, KV_HEADS, ATT_DH), 0.5),
        "cache_kidx": nrm(ks[5], (N_DSA_LAYERS, n_pool, PAGE_SIZE, IDX_DH), 1.0),
        "page_table": page_table,
        "t5_bias": nrm(ks[7], (T5_BUCKETS, ATT_HEADS), 0.5),
        "ret_w_in": nrm(ks[8], (N_RET_LAYERS, D_MODEL, RET_IN), D_MODEL ** -0.5) * ret_col,
        "ret_gn_g": 1.0 + nrm(ks[9], (N_RET_LAYERS, RET_HEADS * RET_DV), 0.02),
        "ret_gn_b": nrm(ks[10], (N_RET_LAYERS, RET_HEADS * RET_DV), 0.02),
        "ret_w_o": nrm(ks[11], (N_RET_LAYERS, RET_HEADS * RET_DV, D_MODEL), (RET_HEADS * RET_DV) ** -0.5 * DEEPNORM_BETA),
        "dsa_w_in": nrm(ks[12], (N_DSA_LAYERS, D_MODEL, DSA_IN), D_MODEL ** -0.5) * dsa_col,
        "dsa_kidx_g": 1.0 + nrm(ks[13], (N_DSA_LAYERS, IDX_DH), 0.02),
        "dsa_kidx_b": nrm(ks[14], (N_DSA_LAYERS, IDX_DH), 0.02),
        "dsa_w_o": nrm(ks[15], (N_DSA_LAYERS, Q_W, D_MODEL), Q_W ** -0.5 * DEEPNORM_BETA),
        "ln_mix_g": 1.0 + nrm(ks[16], (DEPTH, D_MODEL), 0.02),
        "ln_mix_b": nrm(ks[17], (DEPTH, D_MODEL), 0.02),
        "ln_ffn_g": 1.0 + nrm(ks[18], (DEPTH, D_MODEL), 0.02),
        "ln_ffn_b": nrm(ks[19], (DEPTH, D_MODEL), 0.02),
        "moe_w_router": nrm(ks[20], (DEPTH, D_MODEL, N_EXPERTS), D_MODEL ** -0.5),
        "moe_b_router": nrm(ks[21], (DEPTH, N_EXPERTS), 0.01),
        "moe_w_gu": nrm(ks[22], (DEPTH, N_EXPERTS, D_MODEL, 2 * D_EXPERT), D_MODEL ** -0.5),
        "moe_b_gu": nrm(ks[23], (DEPTH, N_EXPERTS, 2 * D_EXPERT), 0.01),
        "moe_w_down": nrm(ks[24], (DEPTH, N_EXPERTS, D_EXPERT, D_MODEL), D_EXPERT ** -0.5 * DEEPNORM_BETA),
        "moe_b_down": nrm(ks[25], (DEPTH, N_EXPERTS, D_MODEL), 0.01),
    }


def reference(x_prompt, x_sample, state_ret, cache_k, cache_v, cache_kidx, page_table, t5_bias,
              ret_w_in, ret_gn_g, ret_gn_b, ret_w_o, dsa_w_in, dsa_kidx_g, dsa_kidx_b, dsa_w_o,
              ln_mix_g, ln_mix_b, ln_ffn_g, ln_ffn_b,
              moe_w_router, moe_b_router, moe_w_gu, moe_b_gu, moe_w_down, moe_b_down):
    xp, xs = x_prompt, x_sample
    ret_p, ret_s = [], []
    k_p, v_p, ik_p, k_s, v_s, ik_s = [], [], [], [], [], []
    for i in range(DEPTH):
        j = i // N_MIXERS
        if i % N_MIXERS == 0:
            zero_state = jnp.zeros((xp.shape[0], RET_HEADS, RET_DK, RET_DV), jnp.float32)
            hp, sp = retention_mixer(xp, zero_state, 0, ret_w_in[j], ret_gn_g[j], ret_gn_b[j], ret_w_o[j])
            hs, ss = retention_mixer(xs, state_ret[j], PAST_LEN, ret_w_in[j], ret_gn_g[j], ret_gn_b[j], ret_w_o[j])
            ret_p.append(sp)
            ret_s.append(ss)
        else:
            hp, kn, vn, ikn = dsa_prompt(xp, dsa_w_in[j], dsa_kidx_g[j], dsa_kidx_b[j], dsa_w_o[j], t5_bias)
            k_p.append(kn)
            v_p.append(vn)
            ik_p.append(ikn)
            hs, kn, vn, ikn = dsa_sample(xs, cache_k[j], cache_v[j], cache_kidx[j], page_table,
                                         dsa_w_in[j], dsa_kidx_g[j], dsa_kidx_b[j], dsa_w_o[j], t5_bias)
            k_s.append(kn)
            v_s.append(vn)
            ik_s.append(ikn)
        xp = _layer_norm(DEEPNORM_ALPHA * xp + hp, ln_mix_g[i], ln_mix_b[i])
        xs = _layer_norm(DEEPNORM_ALPHA * xs + hs, ln_mix_g[i], ln_mix_b[i])
        xp = _layer_norm(DEEPNORM_ALPHA * xp + moe_ffn(xp, moe_w_router[i], moe_b_router[i], moe_w_gu[i],
                                                       moe_b_gu[i], moe_w_down[i], moe_b_down[i]),
                         ln_ffn_g[i], ln_ffn_b[i])
        xs = _layer_norm(DEEPNORM_ALPHA * xs + moe_ffn(xs, moe_w_router[i], moe_b_router[i], moe_w_gu[i],
                                                       moe_b_gu[i], moe_w_down[i], moe_b_down[i]),
                         ln_ffn_g[i], ln_ffn_b[i])
    return (xp, xs, jnp.stack(ret_p), jnp.stack(ret_s), jnp.stack(k_p), jnp.stack(v_p), jnp.stack(ik_p),
            jnp.stack(k_s), jnp.stack(v_s), jnp.stack(ik_s))
```

```python
import functools
import math

import numpy as np
import jax
import jax.numpy as jnp
from jax import lax
from jax.experimental import pallas as pl
from jax.experimental.pallas import tpu as pltpu

F32 = jnp.float32
BF16 = jnp.bfloat16
I32 = jnp.int32

RET_HEADS = 8
RET_CHUNK = 128
ROPE_BASE = 10000.0
ATT_HEADS = 16
ATT_DH = 128
KV_HEADS = 4
ATT_GROUP = ATT_HEADS // KV_HEADS
IDX_HEADS = 16
IDX_DH = 128
TOPK_MAX = 256
T5_BUCKETS = 32
T5_MAX_DIST = 128
N_EXPERTS = 32
TOP_K = 4
SWIGLU_LIMIT = 7.0
SWIGLU_ALPHA = 1.702
DEPTH = 2
DEEPNORM_ALPHA = (2.0 * DEPTH) ** 0.25
LN_EPS = 1e-5

LANES = 128
SUBLANES = 8
VMEM_LIMIT = 56 * 1024 * 1024
NEG_BIG = -1e30
INT_MIN = -(2 ** 31)


def _cparams(sem):
    return pltpu.CompilerParams(dimension_semantics=sem, vmem_limit_bytes=VMEM_LIMIT)


def _ln_rows(x, g, b):
    mu = jnp.mean(x, axis=-1, keepdims=True)
    xc = x - mu
    var = jnp.mean(xc * xc, axis=-1, keepdims=True)
    return xc * lax.rsqrt(var + LN_EPS) * g + b


def _dot(a, b):
    return jnp.dot(a, b, preferred_element_type=F32)


def _dot_nt(a, b):
    return lax.dot_general(a, b, (((1,), (1,)), ((), ())), preferred_element_type=F32)


def _mm_kernel(x_ref, w_ref, o_ref, xb_ref):
    @pl.when(pl.program_id(1) == 0)
    def _():
        xb_ref[...] = x_ref[...].astype(BF16)

    o_ref[...] = _dot(xb_ref[...], w_ref[...].astype(BF16)).astype(o_ref.dtype)


def _matmul(x, w, col0, ncols, tm, tn):
    M, K = x.shape
    assert M % tm == 0 and col0 % tn == 0 and ncols % tn == 0
    c0 = col0 // tn
    return pl.pallas_call(
        _mm_kernel,
        grid=(M // tm, ncols // tn),
        in_specs=[pl.BlockSpec((tm, K), lambda i, j: (i, 0)),
                  pl.BlockSpec((K, tn), lambda i, j: (0, j + c0))],
        out_specs=pl.BlockSpec((tm, tn), lambda i, j: (i, j)),
        out_shape=jax.ShapeDtypeStruct((M, ncols), F32),
        scratch_shapes=[pltpu.VMEM((tm, K), BF16)],
        compiler_params=_cparams(("parallel", "arbitrary")),
    )(x, w)


def _res_ln_kernel(x_ref, h_ref, g_ref, b_ref, o_ref, *, alpha):
    o_ref[...] = _ln_rows(alpha * x_ref[...] + h_ref[...], g_ref[...], b_ref[...])


def _ln_kernel(x_ref, g_ref, b_ref, o_ref):
    o_ref[...] = _ln_rows(x_ref[...], g_ref[...], b_ref[...])


def _res_ln(x, h, g, b, tm):
    M, D = x.shape
    row = pl.BlockSpec((tm, D), lambda i: (i, 0))
    par = pl.BlockSpec((1, D), lambda i: (0, 0))
    return pl.pallas_call(
        functools.partial(_res_ln_kernel, alpha=DEEPNORM_ALPHA),
        grid=(M // tm,),
        in_specs=[row, row, par, par],
        out_specs=row,
        out_shape=jax.ShapeDtypeStruct((M, D), F32),
        compiler_params=_cparams(("parallel",)),
    )(x, h, g.reshape(1, D), b.reshape(1, D))


def _ln_cols(x, col_blk, g, b, tm):
    M = x.shape[0]
    D = LANES
    par = pl.BlockSpec((1, D), lambda i: (0, 0))
    return pl.pallas_call(
        _ln_kernel,
        grid=(M // tm,),
        in_specs=[pl.BlockSpec((tm, D), lambda i: (i, col_blk)), par, par],
        out_specs=pl.BlockSpec((tm, D), lambda i: (i, 0)),
        out_shape=jax.ShapeDtypeStruct((M, D), F32),
        compiler_params=_cparams(("parallel",)),
    )(x, g.reshape(1, D), b.reshape(1, D))


def _ret_kernel(q_ref, k_ref, v_ref, g_ref, cos_ref, sin_ref, intra_ref, qd_ref, kd_ref, sd_ref,
                s0_ref, gng_ref, gnb_ref, o_ref, so_ref, st_ref, *, dk):
    c = pl.program_id(2)
    half = dk // 2

    @pl.when(c == 0)
    def _():
        st_ref[...] = s0_ref[0, 0]

    cos = cos_ref[...]
    sin = sin_ref[...]

    def rot(x):
        x1, x2 = x[:, :half], x[:, half:]
        return jnp.concatenate([x1 * cos - x2 * sin, x1 * sin + x2 * cos], axis=-1)

    q = rot(q_ref[...])
    k = rot(k_ref[...]) * (dk ** -0.5)
    vb = v_ref[...].astype(BF16)
    qb = q.astype(BF16)
    st = st_ref[...]
    scores = _dot_nt(qb, k.astype(BF16)) * intra_ref[0]
    o = _dot(scores.astype(BF16), vb)
    o = o + _dot(qb, st.astype(BF16)) * qd_ref[0]
    kd = (k * kd_ref[0]).astype(BF16)
    new_st = st * sd_ref[0] + lax.dot_general(kd, vb, (((0,), (0,)), ((), ())),
                                              preferred_element_type=F32)
    st_ref[...] = new_st
    so_ref[0, 0] = new_st
    y = _ln_rows(o, gng_ref[0], gnb_ref[0])
    g = g_ref[...]
    o_ref[...] = y * (g * (1.0 / (1.0 + jnp.exp(-g))))


def _ret_log_decay():
    return jnp.log1p(-jnp.exp2(-5.0 - jnp.arange(RET_HEADS, dtype=F32)))


def _retention(h, state0, pos0, B, T, C, Cp, gn_g, gn_b):
    H = RET_HEADS
    DK, DV = state0.shape[2], state0.shape[3]
    NC = T // C
    Tp = NC * Cp
    half = DK // 2
    log_g = _ret_log_decay()
    pos = (pos0 + jnp.arange(Tp, dtype=jnp.int32)).astype(F32)
    inv = ROPE_BASE ** (-jnp.arange(half, dtype=F32) / half)
    ang = pos[:, None] * inv[None, :]
    cos, sin = jnp.cos(ang), jnp.sin(ang)
    p = jnp.arange(C, dtype=F32)
    rel = p[:, None] - p[None, :]
    intra = jnp.where(rel[None] >= 0, jnp.exp(log_g[:, None, None] * jnp.maximum(rel, 0.0)[None]), 0.0)
    qd = jnp.exp((p[:, None] + 1.0) * log_g[None, :]).T[:, :, None]
    kd = jnp.exp((C - 1.0 - p)[:, None] * log_g[None, :]).T[:, :, None]
    sd = jnp.exp(C * log_g)[:, None, None]
    padc = Cp - C
    intra = jnp.pad(intra, ((0, 0), (0, padc), (0, padc)))
    qd = jnp.pad(qd, ((0, 0), (0, padc), (0, 0)))
    kd = jnp.pad(kd, ((0, 0), (0, padc), (0, 0)))

    kb = (H * DK) // DK
    vb = (2 * H * DK) // DV
    gb = vb + H
    row = lambda b, hh, c: b * NC + c
    in_specs = [
        pl.BlockSpec((Cp, DK), lambda b, hh, c: (row(b, hh, c), hh)),
        pl.BlockSpec((Cp, DK), lambda b, hh, c: (row(b, hh, c), kb + hh)),
        pl.BlockSpec((Cp, DV), lambda b, hh, c: (row(b, hh, c), vb + hh)),
        pl.BlockSpec((Cp, DV), lambda b, hh, c: (row(b, hh, c), gb + hh)),
        pl.BlockSpec((Cp, half), lambda b, hh, c: (c, 0)),
        pl.BlockSpec((Cp, half), lambda b, hh, c: (c, 0)),
        pl.BlockSpec((1, Cp, Cp), lambda b, hh, c: (hh, 0, 0)),
        pl.BlockSpec((1, Cp, 1), lambda b, hh, c: (hh, 0, 0)),
        pl.BlockSpec((1, Cp, 1), lambda b, hh, c: (hh, 0, 0)),
        pl.BlockSpec((1, 1, 1), lambda b, hh, c: (hh, 0, 0)),
        pl.BlockSpec((1, 1, DK, DV), lambda b, hh, c: (b, hh, 0, 0)),
        pl.BlockSpec((1, 1, DV), lambda b, hh, c: (hh, 0, 0)),
        pl.BlockSpec((1, 1, DV), lambda b, hh, c: (hh, 0, 0)),
    ]
    out_specs = [
        pl.BlockSpec((Cp, DV), lambda b, hh, c: (row(b, hh, c), hh)),
        pl.BlockSpec((1, 1, DK, DV), lambda b, hh, c: (b, hh, 0, 0)),
    ]
    o, st = pl.pallas_call(
        functools.partial(_ret_kernel, dk=DK),
        grid=(B, H, NC),
        in_specs=in_specs,
        out_specs=out_specs,
        out_shape=[jax.ShapeDtypeStruct((B * Tp, H * DV), F32),
                   jax.ShapeDtypeStruct((B, H, DK, DV), F32)],
        scratch_shapes=[pltpu.VMEM((DK, DV), F32)],
        compiler_params=_cparams(("parallel", "parallel", "arbitrary")),
    )(h, h, h, h, cos, sin, intra, qd, kd, sd, state0,
      gn_g.reshape(H, 1, DV), gn_b.reshape(H, 1, DV))
    return o, st


IDX_SCALE = IDX_HEADS ** -0.5 * IDX_DH ** -0.5


def _idx_scores(iq0, iq1, iw, keys):
    kb = keys.astype(BF16)
    acc = None
    half = IDX_HEADS // 2
    for hh in range(IDX_HEADS):
        src = iq0 if hh < half else iq1
        j = hh % half
        qh = src[:, j * IDX_DH:(j + 1) * IDX_DH].astype(BF16)
        d = jnp.maximum(_dot_nt(qh, kb), 0.0) * (iw[:, hh:hh + 1] * IDX_SCALE)
        acc = d if acc is None else acc + d
    return acc


def _idxp_kernel(iq0_ref, iq1_ref, iw_ref, k_ref, o_ref):
    qi, ki = pl.program_id(1), pl.program_id(2)

    @pl.when(ki <= qi)
    def _():
        o_ref[...] = _idx_scores(iq0_ref[...], iq1_ref[...], iw_ref[...], k_ref[...])

    @pl.when(ki > qi)
    def _():
        o_ref[...] = jnp.zeros_like(o_ref)


def _idxs_kernel(pt_ref, iq0_ref, iq1_ref, iw_ref, kp_ref, kn_ref, o_ref, *, n_pages):
    ki = pl.program_id(1)
    keys = jnp.where(ki < n_pages, kp_ref[0], kn_ref[...])
    o_ref[...] = _idx_scores(iq0_ref[...], iq1_ref[...], iw_ref[...], keys)


def _select_kernel(sc_ref, qpos_ref, o_ref, key_ref, *, topk):
    tr, S = sc_ref.shape
    nblk = S // LANES
    sc = sc_ref[...]
    col = lax.broadcasted_iota(I32, (tr, S), 1)
    valid = col <= qpos_ref[...]
    bits = pltpu.bitcast(sc, I32)
    key = jnp.where(sc == 0.0, 0, bits ^ ((bits >> 31) & 0x7FFFFFFF))
    key_ref[...] = jnp.where(valid, key, INT_MIN)

    def bit_step(i, ans):
        cand = ans | jnp.left_shift(jnp.int32(1), 31 - i)
        cand_s = cand ^ INT_MIN
        cnt = jnp.sum(jnp.where(key_ref[...] >= cand_s, 1.0, 0.0), axis=-1, keepdims=True)
        return jnp.where(cnt >= topk, cand, ans)

    ans = lax.fori_loop(0, 32, bit_step, jnp.zeros((tr, 1), I32))
    thr = ans ^ INT_MIN
    need = topk - jnp.sum(jnp.where(key_ref[...] > thr, 1.0, 0.0), axis=-1, keepdims=True)

    r_i = lax.broadcasted_iota(I32, (LANES, LANES), 0)
    c_i = lax.broadcasted_iota(I32, (LANES, LANES), 1)
    upper = jnp.where(r_i < c_i, 1.0, 0.0).astype(BF16)

    def blk_step(j, carry):
        off = pl.multiple_of(j * LANES, LANES)
        kb = key_ref[:, pl.ds(off, LANES)]
        eq = jnp.where(kb == thr, jnp.where(kb == INT_MIN, 0.0, 1.0), 0.0)
        pre = _dot(eq.astype(BF16), upper) + carry
        take = jnp.where(pre < need, eq, 0.0)
        sel = jnp.where(kb > thr, 1.0, take)
        o_ref[:, pl.ds(off, LANES)] = jnp.where(sel > 0.5, 0.0, NEG_BIG)
        return carry + jnp.sum(eq, axis=-1, keepdims=True)

    lax.fori_loop(0, nblk, blk_step, jnp.zeros((tr, 1), F32))


def _select(scores, qpos, topk, tr):
    R, S = scores.shape
    return pl.pallas_call(
        functools.partial(_select_kernel, topk=topk),
        grid=(R // tr,),
        in_specs=[pl.BlockSpec((tr, S), lambda i: (i, 0)),
                  pl.BlockSpec((tr, 1), lambda i: (i, 0))],
        out_specs=pl.BlockSpec((tr, S), lambda i: (i, 0)),
        out_shape=jax.ShapeDtypeStruct((R, S), F32),
        scratch_shapes=[pltpu.VMEM((tr, S), I32)],
        compiler_params=_cparams(("parallel",)),
    )(scores, qpos)


def _attn_step(q_ref, k, v, mb, tab_ref, cls, first, m_ref, l_ref, acc_ref):
    @pl.when(first)
    def _():
        m_ref[...] = jnp.full_like(m_ref, -3e38)
        l_ref[...] = jnp.zeros_like(l_ref)
        acc_ref[...] = jnp.zeros_like(acc_ref)

    scale = ATT_DH ** -0.5
    for n in range(KV_HEADS):
        kn = k[:, n * ATT_DH:(n + 1) * ATT_DH].astype(BF16)
        vn = v[:, n * ATT_DH:(n + 1) * ATT_DH].astype(BF16)
        for g in range(ATT_GROUP):
            hh = n * ATT_GROUP + g
            qh = q_ref[:, hh * ATT_DH:(hh + 1) * ATT_DH].astype(BF16)
            s = _dot_nt(qh, kn) * scale + tab_ref[cls, hh] + mb
            m_old = m_ref[hh]
            m_new = jnp.maximum(m_old, jnp.max(s, axis=-1, keepdims=True))
            a = jnp.exp(m_old - m_new)
            p = jnp.exp(s - m_new)
            l_ref[hh] = a * l_ref[hh] + jnp.sum(p, axis=-1, keepdims=True)
            acc_ref[hh] = a * acc_ref[hh] + _dot(p.astype(BF16), vn)
            m_ref[hh] = m_new


def _attn_finish(o_ref, l_ref, acc_ref):
    for hh in range(ATT_HEADS):
        o_ref[:, hh * ATT_DH:(hh + 1) * ATT_DH] = acc_ref[hh] / l_ref[hh]


def _attnp_kernel(q_ref, k_ref, v_ref, mb_ref, tab_ref, o_ref, m_ref, l_ref, acc_ref):
    qi, ki = pl.program_id(1), pl.program_id(2)

    @pl.when(ki <= qi)
    def _():
        cls = jnp.minimum(qi - ki, 2)
        _attn_step(q_ref, k_ref[...], v_ref[...], mb_ref[...], tab_ref, cls, ki == 0,
                   m_ref, l_ref, acc_ref)

    @pl.when(ki == pl.num_programs(2) - 1)
    def _():
        _attn_finish(o_ref, l_ref, acc_ref)


def _attns_kernel(pt_ref, q_ref, kp_ref, vp_ref, kn_ref, vn_ref, mb_ref, tab_ref, o_ref,
                  m_ref, l_ref, acc_ref, *, n_pages):
    ki = pl.program_id(1)
    is_page = ki < n_pages
    k = jnp.where(is_page, kp_ref[0], kn_ref[...])
    v = jnp.where(is_page, vp_ref[0], vn_ref[...])
    cls = jnp.minimum(n_pages - ki, 2)
    _attn_step(q_ref, k, v, mb_ref[...], tab_ref, cls, ki == 0, m_ref, l_ref, acc_ref)

    @pl.when(ki == pl.num_programs(1) - 1)
    def _():
        _attn_finish(o_ref, l_ref, acc_ref)


def _t5_bucket(dist):
    max_exact = T5_BUCKETS // 2
    is_small = dist < max_exact
    d = jnp.maximum(dist, 1).astype(F32)
    large = max_exact + (jnp.log(d / max_exact) / math.log(T5_MAX_DIST / max_exact)
                         * (T5_BUCKETS - max_exact)).astype(jnp.int32)
    large = jnp.minimum(large, T5_BUCKETS - 1)
    return jnp.where(is_small, dist, large)


def _bias_table(t5_bias, tq, tk):
    i = jnp.arange(tq, dtype=jnp.int32)[:, None]
    j = jnp.arange(tk, dtype=jnp.int32)[None, :]
    tabs = []
    for c in range(3):
        dist = jnp.maximum(c * tk + i - j, 0)
        tabs.append(jnp.transpose(t5_bias[_t5_bucket(dist)], (2, 0, 1)))
    return jnp.stack(tabs).astype(F32)


def _far_bucket_ok(tk):
    d = np.arange(tk + 1, 8 * tk + 2).astype(np.float32)
    me = T5_BUCKETS // 2
    large = me + (np.log(d / me) / math.log(T5_MAX_DIST / me) * (T5_BUCKETS - me)).astype(np.int32)
    return bool(np.all(np.minimum(large, T5_BUCKETS - 1) == T5_BUCKETS - 1))


Q_W = ATT_HEADS * ATT_DH
KV_W = KV_HEADS * ATT_DH
IQ_W = IDX_HEADS * IDX_DH
DSA_MAIN = Q_W + 2 * KV_W + IQ_W


def _dsa_project(x, w_in, kidx_g, kidx_b, tm):
    h_main = _matmul(x, w_in, 0, DSA_MAIN, tm, 512)
    tail = w_in.shape[1] - DSA_MAIN
    w_tail = jnp.pad(w_in[:, DSA_MAIN:], ((0, 0), (0, 2 * LANES - tail)))
    ikw = _matmul(x, w_tail, 0, 2 * LANES, tm, 2 * LANES)
    ik = _ln_cols(ikw, 0, kidx_g, kidx_b, min(tm, 512))
    return h_main, ik, ikw


def _dsa_prompt(x, w_in, kidx_g, kidx_b, t5_bias, B, T):
    M = B * T
    h_main, ik, ikw = _dsa_project(x, w_in, kidx_g, kidx_b, 1024 if M % 1024 == 0 else M)
    topk = min(TOPK_MAX, T // 4)
    tq = tk = LANES
    assert _far_bucket_ok(tk)
    nq = T // tq
    iqb = Q_W + 2 * KV_W
    assert iqb % (IQ_W // 2) == 0
    scores = pl.pallas_call(
        _idxp_kernel,
        grid=(B, nq, nq),
        in_specs=[pl.BlockSpec((tq, IQ_W // 2), lambda b, qi, ki: (b * nq + qi, iqb // (IQ_W // 2))),
                  pl.BlockSpec((tq, IQ_W // 2), lambda b, qi, ki: (b * nq + qi, iqb // (IQ_W // 2) + 1)),
                  pl.BlockSpec((tq, LANES), lambda b, qi, ki: (b * nq + qi, 1)),
                  pl.BlockSpec((tk, IDX_DH), lambda b, qi, ki: (b * nq + jnp.minimum(ki, qi), 0))],
        out_specs=pl.BlockSpec((tq, tk), lambda b, qi, ki: (b * nq + qi, ki)),
        out_shape=jax.ShapeDtypeStruct((M, T), F32),
        compiler_params=_cparams(("parallel", "parallel", "arbitrary")),
    )(h_main, h_main, ikw, ik)
    qpos = jnp.tile(jnp.arange(T, dtype=jnp.int32), B)[:, None]
    mbias = _select(scores, qpos, topk, LANES)
    tab = _bias_table(t5_bias, tq, tk)
    kcol = Q_W // KV_W
    o = pl.pallas_call(
        _attnp_kernel,
        grid=(B, nq, nq),
        in_specs=[pl.BlockSpec((tq, Q_W), lambda b, qi, ki: (b * nq + qi, 0)),
                  pl.BlockSpec((tk, KV_W), lambda b, qi, ki: (b * nq + jnp.minimum(ki, qi), kcol)),
                  pl.BlockSpec((tk, KV_W), lambda b, qi, ki: (b * nq + jnp.minimum(ki, qi), kcol + 1)),
                  pl.BlockSpec((tq, tk), lambda b, qi, ki: (b * nq + qi, jnp.minimum(ki, qi))),
                  pl.BlockSpec((3, ATT_HEADS, tq, tk), lambda b, qi, ki: (0, 0, 0, 0))],
        out_specs=pl.BlockSpec((tq, Q_W), lambda b, qi, ki: (b * nq + qi, 0)),
        out_shape=jax.ShapeDtypeStruct((M, Q_W), F32),
        scratch_shapes=[pltpu.VMEM((ATT_HEADS, tq, 1), F32),
                        pltpu.VMEM((ATT_HEADS, tq, 1), F32),
                        pltpu.VMEM((ATT_HEADS, tq, ATT_DH), F32)],
        compiler_params=_cparams(("parallel", "parallel", "arbitrary")),
    )(h_main, h_main, h_main, mbias, tab)
    k_new = h_main[:, Q_W:Q_W + KV_W]
    v_new = h_main[:, Q_W + KV_W:Q_W + 2 * KV_W]
    return o, k_new, v_new, ik


def _dsa_sample(x, cache_k, cache_v, cache_kidx, page_table, w_in, kidx_g, kidx_b, t5_bias,
                B, T, Tp, past_len):
    M = B * Tp
    page = cache_kidx.shape[1]
    n_pages = past_len // page
    assert page == LANES and _far_bucket_ok(page)
    h_main, ik, ikw = _dsa_project(x, w_in, kidx_g, kidx_b, M)
    topk = min(TOPK_MAX, (past_len + T) // 4)
    nk = n_pages + 1
    S = nk * page
    n_pool = cache_kidx.shape[0]
    ck = cache_k.reshape(n_pool, page, KV_W)
    cv = cache_v.reshape(n_pool, page, KV_W)
    pt = page_table.reshape(-1).astype(jnp.int32)

    def pad_page(a):
        return jnp.pad(a.reshape(B, Tp, -1), ((0, 0), (0, page - Tp), (0, 0))).reshape(B * page, -1)

    ik_new = pad_page(ik)
    k_new = h_main[:, Q_W:Q_W + KV_W]
    v_new = h_main[:, Q_W + KV_W:Q_W + 2 * KV_W]
    iqb = Q_W + 2 * KV_W
    pidx = lambda b, ki, pt_ref: pt_ref[b * n_pages + jnp.minimum(ki, n_pages - 1)]
    scores = pl.pallas_call(
        functools.partial(_idxs_kernel, n_pages=n_pages),
        grid_spec=pltpu.PrefetchScalarGridSpec(
            num_scalar_prefetch=1,
            grid=(B, nk),
            in_specs=[pl.BlockSpec((Tp, IQ_W // 2), lambda b, ki, p: (b, iqb // (IQ_W // 2))),
                      pl.BlockSpec((Tp, IQ_W // 2), lambda b, ki, p: (b, iqb // (IQ_W // 2) + 1)),
                      pl.BlockSpec((Tp, LANES), lambda b, ki, p: (b, 1)),
                      pl.BlockSpec((1, page, IDX_DH), lambda b, ki, p: (pidx(b, ki, p), 0, 0)),
                      pl.BlockSpec((page, IDX_DH), lambda b, ki, p: (b, 0))],
            out_specs=pl.BlockSpec((Tp, page), lambda b, ki, p: (b, ki)),
        ),
        out_shape=jax.ShapeDtypeStruct((M, S), F32),
        compiler_params=_cparams(("parallel", "arbitrary")),
    )(pt, h_main, h_main, ikw, cache_kidx, ik_new)
    qpos = jnp.tile(past_len + jnp.arange(Tp, dtype=jnp.int32), B)[:, None]
    mbias = _select(scores, qpos, topk, M)
    tab = _bias_table(t5_bias, Tp, page)
    o = pl.pallas_call(
        functools.partial(_attns_kernel, n_pages=n_pages),
        grid_spec=pltpu.PrefetchScalarGridSpec(
            num_scalar_prefetch=1,
            grid=(B, nk),
            in_specs=[pl.BlockSpec((Tp, Q_W), lambda b, ki, p: (b, 0)),
                      pl.BlockSpec((1, page, KV_W), lambda b, ki, p: (pidx(b, ki, p), 0, 0)),
                      pl.BlockSpec((1, page, KV_W), lambda b, ki, p: (pidx(b, ki, p), 0, 0)),
                      pl.BlockSpec((page, KV_W), lambda b, ki, p: (b, 0)),
                      pl.BlockSpec((page, KV_W), lambda b, ki, p: (b, 0)),
                      pl.BlockSpec((Tp, page), lambda b, ki, p: (b, ki)),
                      pl.BlockSpec((3, ATT_HEADS, Tp, page), lambda b, ki, p: (0, 0, 0, 0))],
            out_specs=pl.BlockSpec((Tp, Q_W), lambda b, ki, p: (b, 0)),
            scratch_shapes=[pltpu.VMEM((ATT_HEADS, Tp, 1), F32),
                            pltpu.VMEM((ATT_HEADS, Tp, 1), F32),
                            pltpu.VMEM((ATT_HEADS, Tp, ATT_DH), F32)],
        ),
        out_shape=jax.ShapeDtypeStruct((M, Q_W), F32),
        compiler_params=_cparams(("parallel", "arbitrary")),
    )(pt, h_main, ck, cv, pad_page(k_new), pad_page(v_new), mbias, tab)
    return o, k_new, v_new, ik


def _split_bf16(a):
    hi = a.astype(BF16)
    lo = (a - hi.astype(F32)).astype(BF16)
    return hi, lo


def _router_kernel(x_ref, w_ref, b_ref, e_ref, g_ref, p_ref, cnt_ref, carry_ref):
    i = pl.program_id(0)
    tm = x_ref.shape[0]
    E = N_EXPERTS

    @pl.when(i == 0)
    def _():
        carry_ref[...] = jnp.zeros_like(carry_ref)

    xh, xl = _split_bf16(x_ref[...])
    wh, wl = _split_bf16(w_ref[...])
    logits = _dot(xh, wh) + (_dot(xl, wh) + _dot(xh, wl)) + b_ref[...]
    lane = lax.broadcasted_iota(I32, (tm, E), 1).astype(F32)
    lane_o = lax.broadcasted_iota(I32, (tm, LANES), 1)
    vals, idxs = [], []
    cur = logits
    onehot = jnp.zeros((tm, E), F32)
    for _ in range(TOP_K):
        mx = jnp.max(cur, axis=-1, keepdims=True)
        ix = jnp.min(jnp.where(cur == mx, lane, float(E)), axis=-1, keepdims=True)
        hit = lane == ix
        onehot = jnp.where(hit, 1.0, onehot)
        cur = jnp.where(hit, -jnp.inf, cur)
        vals.append(mx)
        idxs.append(ix)
    ex = [jnp.exp(v - vals[0]) for v in vals]
    den = ex[0] + ex[1] + ex[2] + ex[3]
    r_i = lax.broadcasted_iota(I32, (tm, tm), 0)
    c_i = lax.broadcasted_iota(I32, (tm, tm), 1)
    lower = jnp.where(c_i < r_i, 1.0, 0.0).astype(BF16)
    prefix = _dot(lower, onehot.astype(BF16)) + carry_ref[...]
    e_out = jnp.zeros((tm, LANES), I32)
    g_out = jnp.zeros((tm, LANES), F32)
    p_out = jnp.zeros((tm, LANES), I32)
    for k in range(TOP_K):
        pos = jnp.sum(jnp.where(lane == idxs[k], prefix, 0.0), axis=-1, keepdims=True)
        e_out = jnp.where(lane_o == k, idxs[k].astype(I32), e_out)
        g_out = jnp.where(lane_o == k, ex[k] / den, g_out)
        p_out = jnp.where(lane_o == k, pos.astype(I32), p_out)
    e_ref[...] = e_out
    g_ref[...] = g_out
    p_ref[...] = p_out
    carry_ref[...] = carry_ref[...] + jnp.sum(onehot, axis=0, keepdims=True)
    cnt_ref[...] = carry_ref[...].astype(I32)


def _router(x, w_router, b_router, layer, tm):
    T, D = x.shape
    E = N_EXPERTS
    outs = pl.pallas_call(
        _router_kernel,
        grid=(T // tm,),
        in_specs=[pl.BlockSpec((tm, D), lambda i: (i, 0)),
                  pl.BlockSpec((None, D, E), lambda i: (layer, 0, 0)),
                  pl.BlockSpec((None, 1, E), lambda i: (layer, 0, 0))],
        out_specs=[pl.BlockSpec((tm, LANES), lambda i: (i, 0)),
                   pl.BlockSpec((tm, LANES), lambda i: (i, 0)),
                   pl.BlockSpec((tm, LANES), lambda i: (i, 0)),
                   pl.BlockSpec((1, E), lambda i: (0, 0))],
        out_shape=[jax.ShapeDtypeStruct((T, LANES), I32),
                   jax.ShapeDtypeStruct((T, LANES), F32),
                   jax.ShapeDtypeStruct((T, LANES), I32),
                   jax.ShapeDtypeStruct((1, E), I32)],
        scratch_shapes=[pltpu.VMEM((1, E), F32)],
        compiler_params=_cparams(("arbitrary",)),
    )(x, w_router, b_router.reshape(-1, 1, E))
    return outs


def _row_copy(src, s_row, dst, d_row, sem):
    return pltpu.make_async_copy(src.at[pl.ds(s_row, 1)], dst.at[pl.ds(d_row, 1)], sem)


def _dispatch_kernel(dest_ref, x_hbm, init_hbm, xs_hbm, sem, *, tb):
    del init_hbm
    i = pl.program_id(0)

    def issue(r, c):
        for k in range(TOP_K):
            _row_copy(x_hbm, i * tb + r, xs_hbm, dest_ref[r * TOP_K + k], sem).start()
        return c

    lax.fori_loop(0, tb, issue, 0)

    def drain(r, c):
        for k in range(TOP_K):
            _row_copy(x_hbm, 0, xs_hbm, 0, sem).wait()
        return c

    lax.fori_loop(0, tb, drain, 0)


def _dispatch(x, dest_flat, n_rows, tb):
    T, D = x.shape
    init = jnp.zeros((n_rows, D), x.dtype)
    return pl.pallas_call(
        functools.partial(_dispatch_kernel, tb=tb),
        grid=(T // tb,),
        in_specs=[pl.BlockSpec((tb * TOP_K,), lambda i: (i,), memory_space=pltpu.SMEM),
                  pl.BlockSpec(memory_space=pl.ANY),
                  pl.BlockSpec(memory_space=pl.ANY)],
        out_specs=pl.BlockSpec(memory_space=pl.ANY),
        out_shape=jax.ShapeDtypeStruct((n_rows, D), x.dtype),
        scratch_shapes=[pltpu.SemaphoreType.DMA(())],
        input_output_aliases={2: 0},
        compiler_params=_cparams(("arbitrary",)),
    )(dest_flat, x, init)


def _gu_kernel(be_ref, nb_ref, x_ref, wg_ref, wu_ref, bg_ref, bu_ref, o_ref, wgb_ref, wub_ref):
    rb = pl.program_id(1)
    prev = be_ref[jnp.maximum(rb - 1, 0)]
    changed = jnp.logical_or(rb == 0, be_ref[rb] != prev)

    @pl.when(jnp.logical_and(changed, rb < nb_ref[0]))
    def _():
        wgb_ref[...] = wg_ref[...].astype(BF16)
        wub_ref[...] = wu_ref[...].astype(BF16)

    @pl.when(rb < nb_ref[0])
    def _():
        xb = x_ref[...].astype(BF16)
        gate = _dot(xb, wgb_ref[...]) + bg_ref[...]
        up = _dot(xb, wub_ref[...]) + bu_ref[...]
        gate = jnp.minimum(gate, SWIGLU_LIMIT)
        up = jnp.clip(up, -SWIGLU_LIMIT, SWIGLU_LIMIT)
        sig = 1.0 / (1.0 + jnp.exp(-(gate * SWIGLU_ALPHA)))
        o_ref[...] = ((up + 1.0) * (gate * sig)).astype(o_ref.dtype)

    @pl.when(rb >= nb_ref[0])
    def _():
        o_ref[...] = jnp.zeros_like(o_ref)


def _down_kernel(be_ref, nb_ref, h_ref, w_ref, b_ref, o_ref, wb_ref):
    rb = pl.program_id(1)
    prev = be_ref[jnp.maximum(rb - 1, 0)]
    changed = jnp.logical_or(rb == 0, be_ref[rb] != prev)

    @pl.when(jnp.logical_and(changed, rb < nb_ref[0]))
    def _():
        wb_ref[...] = w_ref[...].astype(BF16)

    @pl.when(rb < nb_ref[0])
    def _():
        o_ref[...] = _dot(h_ref[...], wb_ref[...]) + b_ref[...]

    @pl.when(rb >= nb_ref[0])
    def _():
        o_ref[...] = jnp.zeros_like(o_ref)


def _experts(xs, block_e, n_used, w_gu, b_gu, w_down, b_down, layer, tm, tn):
    n_rows, D = xs.shape
    DE = w_down.shape[2]
    NB = n_rows // tm
    ng = DE // tn
    rbc = lambda rb, nb: jnp.minimum(rb, nb[0] - 1)
    hid = pl.pallas_call(
        _gu_kernel,
        grid_spec=pltpu.PrefetchScalarGridSpec(
            num_scalar_prefetch=2,
            grid=(ng, NB),
            in_specs=[pl.BlockSpec((tm, D), lambda n, rb, be, nb: (rbc(rb, nb), 0)),
                      pl.BlockSpec((None, None, D, tn), lambda n, rb, be, nb: (layer, be[rbc(rb, nb)], 0, n)),
                      pl.BlockSpec((None, None, D, tn), lambda n, rb, be, nb: (layer, be[rbc(rb, nb)], 0, ng + n)),
                      pl.BlockSpec((None, None, 1, tn), lambda n, rb, be, nb: (layer, be[rbc(rb, nb)], 0, n)),
                      pl.BlockSpec((None, None, 1, tn), lambda n, rb, be, nb: (layer, be[rbc(rb, nb)], 0, ng + n))],
            out_specs=pl.BlockSpec((tm, tn), lambda n, rb, be, nb: (rb, n)),
            scratch_shapes=[pltpu.VMEM((D, tn), BF16), pltpu.VMEM((D, tn), BF16)],
        ),
        out_shape=jax.ShapeDtypeStruct((n_rows, DE), BF16),
        compiler_params=_cparams(("arbitrary", "arbitrary")),
    )(block_e, n_used, xs, w_gu, w_gu, b_gu.reshape(DEPTH, N_EXPERTS, 1, 2 * DE),
      b_gu.reshape(DEPTH, N_EXPERTS, 1, 2 * DE))
    nd = D // tn
    out = pl.pallas_call(
        _down_kernel,
        grid_spec=pltpu.PrefetchScalarGridSpec(
            num_scalar_prefetch=2,
            grid=(nd, NB),
            in_specs=[pl.BlockSpec((tm, DE), lambda n, rb, be, nb: (rbc(rb, nb), 0)),
                      pl.BlockSpec((None, None, DE, tn), lambda n, rb, be, nb: (layer, be[rbc(rb, nb)], 0, n)),
                      pl.BlockSpec((None, None, 1, tn), lambda n, rb, be, nb: (layer, be[rbc(rb, nb)], 0, n))],
            out_specs=pl.BlockSpec((tm, tn), lambda n, rb, be, nb: (rb, n)),
            scratch_shapes=[pltpu.VMEM((DE, tn), BF16)],
        ),
        out_shape=jax.ShapeDtypeStruct((n_rows, D), F32),
        compiler_params=_cparams(("arbitrary", "arbitrary")),
    )(block_e, n_used, hid, w_down, b_down.reshape(DEPTH, N_EXPERTS, 1, D))
    return out


def _combine_kernel(dest_ref, g_ref, x_ref, rows_hbm, lg_ref, lb_ref, o_ref, buf_ref, sem, *, tb):
    def issue(r, c):
        for k in range(TOP_K):
            _row_copy(rows_hbm, dest_ref[r * TOP_K + k], buf_ref.at[k], r, sem).start()
        return c

    lax.fori_loop(0, tb, issue, 0)

    def drain(r, c):
        for k in range(TOP_K):
            _row_copy(rows_hbm, 0, buf_ref.at[k], 0, sem).wait()
        return c

    lax.fori_loop(0, tb, drain, 0)
    g = g_ref[...]
    y = buf_ref[0] * g[:, 0:1]
    for k in range(1, TOP_K):
        y = y + buf_ref[k] * g[:, k:k + 1]
    o_ref[...] = _ln_rows(DEEPNORM_ALPHA * x_ref[...] + y, lg_ref[...], lb_ref[...])


def _combine(x, rows, dest_flat, gates, ln_g, ln_b, tb):
    T, D = x.shape
    return pl.pallas_call(
        functools.partial(_combine_kernel, tb=tb),
        grid=(T // tb,),
        in_specs=[pl.BlockSpec((tb * TOP_K,), lambda i: (i,), memory_space=pltpu.SMEM),
                  pl.BlockSpec((tb, LANES), lambda i: (i, 0)),
                  pl.BlockSpec((tb, D), lambda i: (i, 0)),
                  pl.BlockSpec(memory_space=pl.ANY),
                  pl.BlockSpec((1, D), lambda i: (0, 0)),
                  pl.BlockSpec((1, D), lambda i: (0, 0))],
        out_specs=pl.BlockSpec((tb, D), lambda i: (i, 0)),
        out_shape=jax.ShapeDtypeStruct((T, D), F32),
        scratch_shapes=[pltpu.VMEM((TOP_K, tb, D), F32), pltpu.SemaphoreType.DMA(())],
        compiler_params=_cparams(("arbitrary",)),
    )(dest_flat, gates, x, rows, ln_g.reshape(1, D), ln_b.reshape(1, D))


def _moe_ln(x, w_router, b_router, w_gu, b_gu, w_down, b_down, ln_g, ln_b, layer, tm_r, tm, tb):
    T, D = x.shape
    E = N_EXPERTS
    top_e, gates, pos, counts = _router(x, w_router, b_router, layer, tm_r)
    counts = counts[0]
    padded = ((counts + tm - 1) // tm) * tm
    pend = jnp.cumsum(padded)
    pstart = pend - padded
    e4 = top_e[:, :TOP_K]
    dest = (pstart[e4] + pos[:, :TOP_K]).reshape(-1).astype(jnp.int32)
    nb_max = (T * TOP_K) // tm + E
    block_e = jnp.minimum(jnp.searchsorted(pend, jnp.arange(nb_max, dtype=jnp.int32) * tm, side='right'),
                          E - 1).astype(jnp.int32)
    n_used = (pend[-1] // tm).astype(jnp.int32).reshape(1)
    xs = _dispatch(x, dest, nb_max * tm, tb)
    rows = _experts(xs, block_e, n_used, w_gu, b_gu, w_down, b_down, layer, tm, 512)
    return _combine(x, rows, dest, gates, ln_g, ln_b, tb)


def kernel(x_prompt, x_sample, state_ret, cache_k, cache_v, cache_kidx, page_table, t5_bias, ret_w_in, ret_gn_g, ret_gn_b, ret_w_o, dsa_w_in, dsa_kidx_g, dsa_kidx_b, dsa_w_o, ln_mix_g, ln_mix_b, ln_ffn_g, ln_ffn_b, moe_w_router, moe_b_router, moe_w_gu, moe_b_gu, moe_w_down, moe_b_down):
    B, T, D = x_prompt.shape
    Bs, Ts, _ = x_sample.shape
    Tsp = SUBLANES
    past_len = page_table.shape[1] * cache_k.shape[2]
    Mp = B * T
    Ms = Bs * Tsp

    def pad_s(a):
        return jnp.pad(a.reshape(Bs, Ts, -1), ((0, 0), (0, Tsp - Ts), (0, 0))).reshape(Ms, -1)

    def unpad_s(a):
        return a.reshape(Bs, Tsp, -1)[:, :Ts].reshape(Bs * Ts, -1)

    xp = x_prompt.reshape(Mp, D)
    xs = x_sample.reshape(Bs * Ts, D)
    tm_p = 1024 if Mp % 1024 == 0 else Mp
    C = RET_CHUNK if T % RET_CHUNK == 0 else T

    moe = lambda x, i, tm_r, tm, tb: _moe_ln(
        x, moe_w_router, moe_b_router, moe_w_gu, moe_b_gu, moe_w_down, moe_b_down,
        ln_ffn_g[i], ln_ffn_b[i], i, tm_r, tm, tb)

    w_in, w_o = ret_w_in[0], ret_w_o[0]
    hp = _matmul(xp, w_in, 0, w_in.shape[1], tm_p, 512)
    op, ret_p = _retention(hp, jnp.zeros((B,) + state_ret.shape[2:], F32), 0, B, T, C, C,
                           ret_gn_g[0], ret_gn_b[0])
    mp = _matmul(op, w_o, 0, D, 512, 512)
    xp = _res_ln(xp, mp, ln_mix_g[0], ln_mix_b[0], 256)

    xs_pad = pad_s(xs)
    hs = _matmul(xs_pad, w_in, 0, w_in.shape[1], Ms, 512)
    os_, ret_s = _retention(hs, state_ret[0], past_len, Bs, Ts, Ts, Tsp, ret_gn_g[0], ret_gn_b[0])
    ms = _matmul(os_, w_o, 0, D, Ms, 512)
    xs = unpad_s(_res_ln(xs_pad, ms, ln_mix_g[0], ln_mix_b[0], Ms))

    xp = moe(xp, 0, 256, 256, 256)
    xs = moe(xs, 0, Bs * Ts, 16, Bs * Ts)

    w_in, w_o = dsa_w_in[0], dsa_w_o[0]
    ap, k_p, v_p, ik_p = _dsa_prompt(xp, w_in, dsa_kidx_g[0], dsa_kidx_b[0], t5_bias, B, T)
    mp = _matmul(ap, w_o, 0, D, tm_p, 512)
    xp = _res_ln(xp, mp, ln_mix_g[1], ln_mix_b[1], 256)

    xs_pad = pad_s(xs)
    as_, k_s, v_s, ik_s = _dsa_sample(xs_pad, cache_k[0], cache_v[0], cache_kidx[0], page_table, w_in,
                                      dsa_kidx_g[0], dsa_kidx_b[0], t5_bias, Bs, Ts, Tsp, past_len)
    ms = _matmul(as_, w_o, 0, D, Ms, 512)
    xs = unpad_s(_res_ln(xs_pad, ms, ln_mix_g[1], ln_mix_b[1], Ms))

    xp = moe(xp, 1, 256, 256, 256)
    xs = moe(xs, 1, Bs * Ts, 16, Bs * Ts)

    return (xp.reshape(B, T, D), xs.reshape(Bs, Ts, D),
            ret_p[None], ret_s[None],
            k_p.reshape(1, B, T, KV_HEADS, ATT_DH), v_p.reshape(1, B, T, KV_HEADS, ATT_DH),
            ik_p.reshape(1, B, T, IDX_DH),
            unpad_s(k_s).reshape(1, Bs, Ts, KV_HEADS, ATT_DH),
            unpad_s(v_s).reshape(1, Bs, Ts, KV_HEADS, ATT_DH),
            unpad_s(ik_s).reshape(1, Bs, Ts, IDX_DH))
```

```python
import functools
import math

import numpy as np
import jax
import jax.numpy as jnp
from jax import lax
from jax.experimental import pallas as pl
from jax.experimental.pallas import tpu as pltpu

F32 = jnp.float32
BF16 = jnp.bfloat16
I32 = jnp.int32

RET_HEADS = 8
RET_CHUNK = 128
ROPE_BASE = 10000.0
ATT_HEADS = 16
ATT_DH = 128
KV_HEADS = 4
ATT_GROUP = ATT_HEADS // KV_HEADS
IDX_HEADS = 16
IDX_DH = 128
TOPK_MAX = 256
T5_BUCKETS = 32
T5_MAX_DIST = 128
N_EXPERTS = 32
TOP_K = 4
SWIGLU_LIMIT = 7.0
SWIGLU_ALPHA = 1.702
DEPTH = 2
DEEPNORM_ALPHA = (2.0 * DEPTH) ** 0.25
LN_EPS = 1e-5

LANES = 128
SUBLANES = 8
VMEM_LIMIT = 56 * 1024 * 1024
NEG_BIG = -1e30
INT_MIN = -(2 ** 31)


def _cparams(sem):
    return pltpu.CompilerParams(dimension_semantics=sem, vmem_limit_bytes=VMEM_LIMIT)


def _ln_rows(x, g, b):
    mu = jnp.mean(x, axis=-1, keepdims=True)
    xc = x - mu
    var = jnp.mean(xc * xc, axis=-1, keepdims=True)
    return xc * lax.rsqrt(var + LN_EPS) * g + b


def _dot(a, b):
    return jnp.dot(a, b, preferred_element_type=F32)


def _dot_nt(a, b):
    return lax.dot_general(a, b, (((1,), (1,)), ((), ())), preferred_element_type=F32)


def _mm_kernel(x_ref, w_ref, o_ref, xb_ref):
    @pl.when(pl.program_id(1) == 0)
    def _():
        xb_ref[...] = x_ref[...].astype(BF16)

    o_ref[...] = _dot(xb_ref[...], w_ref[...].astype(BF16)).astype(o_ref.dtype)


def _matmul(x, w, col0, ncols, tm, tn):
    M, K = x.shape
    assert M % tm == 0 and col0 % tn == 0 and ncols % tn == 0
    c0 = col0 // tn
    return pl.pallas_call(
        _mm_kernel,
        grid=(M // tm, ncols // tn),
        in_specs=[pl.BlockSpec((tm, K), lambda i, j: (i, 0)),
                  pl.BlockSpec((K, tn), lambda i, j: (0, j + c0))],
        out_specs=pl.BlockSpec((tm, tn), lambda i, j: (i, j)),
        out_shape=jax.ShapeDtypeStruct((M, ncols), F32),
        scratch_shapes=[pltpu.VMEM((tm, K), BF16)],
        compiler_params=_cparams(("parallel", "arbitrary")),
    )(x, w)


def _res_ln_kernel(x_ref, h_ref, g_ref, b_ref, o_ref, *, alpha):
    o_ref[...] = _ln_rows(alpha * x_ref[...] + h_ref[...], g_ref[...], b_ref[...])


def _ln_kernel(x_ref, g_ref, b_ref, o_ref):
    o_ref[...] = _ln_rows(x_ref[...], g_ref[...], b_ref[...])


def _res_ln(x, h, g, b, tm):
    M, D = x.shape
    row = pl.BlockSpec((tm, D), lambda i: (i, 0))
    par = pl.BlockSpec((1, D), lambda i: (0, 0))
    return pl.pallas_call(
        functools.partial(_res_ln_kernel, alpha=DEEPNORM_ALPHA),
        grid=(M // tm,),
        in_specs=[row, row, par, par],
        out_specs=row,
        out_shape=jax.ShapeDtypeStruct((M, D), F32),
        compiler_params=_cparams(("parallel",)),
    )(x, h, g.reshape(1, D), b.reshape(1, D))


def _ln_cols(x, col_blk, g, b, tm):
    M = x.shape[0]
    D = LANES
    par = pl.BlockSpec((1, D), lambda i: (0, 0))
    return pl.pallas_call(
        _ln_kernel,
        grid=(M // tm,),
        in_specs=[pl.BlockSpec((tm, D), lambda i: (i, col_blk)), par, par],
        out_specs=pl.BlockSpec((tm, D), lambda i: (i, 0)),
        out_shape=jax.ShapeDtypeStruct((M, D), F32),
        compiler_params=_cparams(("parallel",)),
    )(x, g.reshape(1, D), b.reshape(1, D))


def _ret_kernel(q_ref, k_ref, v_ref, g_ref, cos_ref, sin_ref, intra_ref, qd_ref, kd_ref, sd_ref,
                s0_ref, gng_ref, gnb_ref, o_ref, so_ref, st_ref, *, dk):
    c = pl.program_id(2)
    half = dk // 2

    @pl.when(c == 0)
    def _():
        st_ref[...] = s0_ref[0, 0]

    cos = cos_ref[...]
    sin = sin_ref[...]

    def rot(x):
        x1, x2 = x[:, :half], x[:, half:]
        return jnp.concatenate([x1 * cos - x2 * sin, x1 * sin + x2 * cos], axis=-1)

    q = rot(q_ref[...])
    k = rot(k_ref[...]) * (dk ** -0.5)
    vb = v_ref[...].astype(BF16)
    qb = q.astype(BF16)
    st = st_ref[...]
    scores = _dot_nt(qb, k.astype(BF16)) * intra_ref[0]
    o = _dot(scores.astype(BF16), vb)
    o = o + _dot(qb, st.astype(BF16)) * qd_ref[0]
    kd = (k * kd_ref[0]).astype(BF16)
    new_st = st * sd_ref[0] + lax.dot_general(kd, vb, (((0,), (0,)), ((), ())),
                                              preferred_element_type=F32)
    st_ref[...] = new_st
    so_ref[0, 0] = new_st
    y = _ln_rows(o, gng_ref[0], gnb_ref[0])
    g = g_ref[...]
    o_ref[...] = y * (g * (1.0 / (1.0 + jnp.exp(-g))))


def _ret_log_decay():
    return jnp.log1p(-jnp.exp2(-5.0 - jnp.arange(RET_HEADS, dtype=F32)))


def _retention(h, state0, pos0, B, T, C, Cp, gn_g, gn_b):
    H = RET_HEADS
    DK, DV = state0.shape[2], state0.shape[3]
    NC = T // C
    Tp = NC * Cp
    half = DK // 2
    log_g = _ret_log_decay()
    pos = (pos0 + jnp.arange(Tp, dtype=jnp.int32)).astype(F32)
    inv = ROPE_BASE ** (-jnp.arange(half, dtype=F32) / half)
    ang = pos[:, None] * inv[None, :]
    cos, sin = jnp.cos(ang), jnp.sin(ang)
    p = jnp.arange(C, dtype=F32)
    rel = p[:, None] - p[None, :]
    intra = jnp.where(rel[None] >= 0, jnp.exp(log_g[:, None, None] * jnp.maximum(rel, 0.0)[None]), 0.0)
    qd = jnp.exp((p[:, None] + 1.0) * log_g[None, :]).T[:, :, None]
    kd = jnp.exp((C - 1.0 - p)[:, None] * log_g[None, :]).T[:, :, None]
    sd = jnp.exp(C * log_g)[:, None, None]
    padc = Cp - C
    intra = jnp.pad(intra, ((0, 0), (0, padc), (0, padc)))
    qd = jnp.pad(qd, ((0, 0), (0, padc), (0, 0)))
    kd = jnp.pad(kd, ((0, 0), (0, padc), (0, 0)))

    kb = (H * DK) // DK
    vb = (2 * H * DK) // DV
    gb = vb + H
    row = lambda b, hh, c: b * NC + c
    in_specs = [
        pl.BlockSpec((Cp, DK), lambda b, hh, c: (row(b, hh, c), hh)),
        pl.BlockSpec((Cp, DK), lambda b, hh, c: (row(b, hh, c), kb + hh)),
        pl.BlockSpec((Cp, DV), lambda b, hh, c: (row(b, hh, c), vb + hh)),
        pl.BlockSpec((Cp, DV), lambda b, hh, c: (row(b, hh, c), gb + hh)),
        pl.BlockSpec((Cp, half), lambda b, hh, c: (c, 0)),
        pl.BlockSpec((Cp, half), lambda b, hh, c: (c, 0)),
        pl.BlockSpec((1, Cp, Cp), lambda b, hh, c: (hh, 0, 0)),
        pl.BlockSpec((1, Cp, 1), lambda b, hh, c: (hh, 0, 0)),
        pl.BlockSpec((1, Cp, 1), lambda b, hh, c: (hh, 0, 0)),
        pl.BlockSpec((1, 1, 1), lambda b, hh, c: (hh, 0, 0)),
        pl.BlockSpec((1, 1, DK, DV), lambda b, hh, c: (b, hh, 0, 0)),
        pl.BlockSpec((1, 1, DV), lambda b, hh, c: (hh, 0, 0)),
        pl.BlockSpec((1, 1, DV), lambda b, hh, c: (hh, 0, 0)),
    ]
    out_specs = [
        pl.BlockSpec((Cp, DV), lambda b, hh, c: (row(b, hh, c), hh)),
        pl.BlockSpec((1, 1, DK, DV), lambda b, hh, c: (b, hh, 0, 0)),
    ]
    o, st = pl.pallas_call(
        functools.partial(_ret_kernel, dk=DK),
        grid=(B, H, NC),
        in_specs=in_specs,
        out_specs=out_specs,
        out_shape=[jax.ShapeDtypeStruct((B * Tp, H * DV), F32),
                   jax.ShapeDtypeStruct((B, H, DK, DV), F32)],
        scratch_shapes=[pltpu.VMEM((DK, DV), F32)],
        compiler_params=_cparams(("parallel", "parallel", "arbitrary")),
    )(h, h, h, h, cos, sin, intra, qd, kd, sd, state0,
      gn_g.reshape(H, 1, DV), gn_b.reshape(H, 1, DV))
    return o, st


IDX_SCALE = IDX_HEADS ** -0.5 * IDX_DH ** -0.5


def _select_rows(sc, qpos, topk, key_ref, o_ref):
    tr, S = sc.shape
    nblk = S // LANES
    col = lax.broadcasted_iota(I32, (tr, S), 1)
    valid = col <= qpos
    bits = pltpu.bitcast(sc, I32)
    key = jnp.where(sc == 0.0, 0, bits ^ ((bits >> 31) & 0x7FFFFFFF))
    key_ref[...] = jnp.where(valid, key, INT_MIN)

    def bit_step(i, ans):
        cand = ans | jnp.left_shift(jnp.int32(1), 31 - i)
        cand_s = cand ^ INT_MIN
        cnt = jnp.sum(jnp.where(key_ref[...] >= cand_s, 1.0, 0.0), axis=-1, keepdims=True)
        return jnp.where(cnt >= topk, cand, ans)

    ans = lax.fori_loop(0, 32, bit_step, jnp.zeros((tr, 1), I32))
    thr = ans ^ INT_MIN
    need = topk - jnp.sum(jnp.where(key_ref[...] > thr, 1.0, 0.0), axis=-1, keepdims=True)

    r_i = lax.broadcasted_iota(I32, (LANES, LANES), 0)
    c_i = lax.broadcasted_iota(I32, (LANES, LANES), 1)
    upper = jnp.where(r_i < c_i, 1.0, 0.0).astype(BF16)

    def blk_step(j, carry):
        off = pl.multiple_of(j * LANES, LANES)
        kb = key_ref[:, pl.ds(off, LANES)]
        eq = jnp.where(kb == thr, jnp.where(kb == INT_MIN, 0.0, 1.0), 0.0)
        pre = _dot(eq.astype(BF16), upper) + carry
        take = jnp.where(pre < need, eq, 0.0)
        sel = jnp.where(kb > thr, 1.0, take)
        o_ref[:, pl.ds(off, LANES)] = jnp.where(sel > 0.5, 0.0, NEG_BIG)
        return carry + jnp.sum(eq, axis=-1, keepdims=True)

    lax.fori_loop(0, nblk, blk_step, jnp.zeros((tr, 1), F32))


def _select_kernel(sc_ref, qpos_ref, o_ref, key_ref, *, topk):
    _select_rows(sc_ref[...], qpos_ref[...], topk, key_ref, o_ref)


def _select(scores, qpos, topk, tr):
    R, S = scores.shape
    return pl.pallas_call(
        functools.partial(_select_kernel, topk=topk),
        grid=(R // tr,),
        in_specs=[pl.BlockSpec((tr, S), lambda i: (i, 0)),
                  pl.BlockSpec((tr, 1), lambda i: (i, 0))],
        out_specs=pl.BlockSpec((tr, S), lambda i: (i, 0)),
        out_shape=jax.ShapeDtypeStruct((R, S), F32),
        scratch_shapes=[pltpu.VMEM((tr, S), I32)],
        compiler_params=_cparams(("parallel",)),
    )(scores, qpos)


def _t5_bucket(dist):
    max_exact = T5_BUCKETS // 2
    is_small = dist < max_exact
    d = jnp.maximum(dist, 1).astype(F32)
    large = max_exact + (jnp.log(d / max_exact) / math.log(T5_MAX_DIST / max_exact)
                         * (T5_BUCKETS - max_exact)).astype(jnp.int32)
    large = jnp.minimum(large, T5_BUCKETS - 1)
    return jnp.where(is_small, dist, large)


def _bias_table(t5_bias, tq, tk):
    i = jnp.arange(tq, dtype=jnp.int32)[:, None]
    j = jnp.arange(tk, dtype=jnp.int32)[None, :]
    tabs = []
    for c in range(3):
        dist = jnp.maximum(c * tk + i - j, 0)
        tabs.append(jnp.transpose(t5_bias[_t5_bucket(dist)], (2, 0, 1)))
    return jnp.stack(tabs).astype(F32).reshape(3, KV_HEADS, ATT_GROUP * tq, tk)


def _far_bucket_ok(tk):
    d = np.arange(tk + 1, 8 * tk + 2).astype(np.float32)
    me = T5_BUCKETS // 2
    large = me + (np.log(d / me) / math.log(T5_MAX_DIST / me) * (T5_BUCKETS - me)).astype(np.int32)
    return bool(np.all(np.minimum(large, T5_BUCKETS - 1) == T5_BUCKETS - 1))


Q_W = ATT_HEADS * ATT_DH
KV_W = KV_HEADS * ATT_DH
IQ_W = IDX_HEADS * IDX_DH
DSA_MAIN = Q_W + 2 * KV_W + IQ_W
ATT_SCALE = ATT_DH ** -0.5


def _dsa_project(x, w_in, kidx_g, kidx_b, tm):
    h_main = _matmul(x, w_in, 0, DSA_MAIN, tm, 512)
    tail = w_in.shape[1] - DSA_MAIN
    w_tail = jnp.pad(w_in[:, DSA_MAIN:], ((0, 0), (0, 2 * LANES - tail)))
    ikw = _matmul(x, w_tail, 0, 2 * LANES, tm, 2 * LANES)
    ik = _ln_cols(ikw, 0, kidx_g, kidx_b, min(tm, 512))
    return h_main, ik, ikw


def _dsap_kernel(iq0_ref, iq1_ref, iw_ref, ik_ref, q_ref, k_ref, v_ref, tab_ref, o_ref,
                 iqs_ref, iwb_ref, sc_ref, key_ref, mb_ref, qs_ref, s_ref, m_ref, l_ref, acc_ref,
                 *, topk):
    qi = pl.program_id(1)
    tq = q_ref.shape[0]
    half = IDX_HEADS // 2
    nb = qi + 1

    iw = iw_ref[...]
    for hh in range(IDX_HEADS):
        src = iq0_ref if hh < half else iq1_ref
        j = hh % half
        iqs_ref[hh * tq:(hh + 1) * tq, :] = src[:, j * IDX_DH:(j + 1) * IDX_DH].astype(BF16)
        iwb_ref[hh * tq:(hh + 1) * tq, :] = jnp.broadcast_to(iw[:, hh:hh + 1] * IDX_SCALE, (tq, LANES))

    sc_ref[...] = jnp.zeros_like(sc_ref)

    def sc_blk(kb, c):
        off = pl.multiple_of(kb * LANES, LANES)
        ikb = ik_ref[pl.ds(off, LANES), :].astype(BF16)
        r = jnp.maximum(_dot_nt(iqs_ref[...], ikb), 0.0) * iwb_ref[...]
        acc = r[0:tq]
        for hh in range(1, IDX_HEADS):
            acc = acc + r[hh * tq:(hh + 1) * tq]
        sc_ref[:, pl.ds(off, LANES)] = acc
        return c

    lax.fori_loop(0, nb, sc_blk, 0)

    qpos = qi * tq + lax.broadcasted_iota(I32, (tq, 1), 0)
    _select_rows(sc_ref[...], qpos, topk, key_ref, mb_ref)

    for n in range(KV_HEADS):
        for g in range(ATT_GROUP):
            hh = n * ATT_GROUP + g
            qs_ref[g * tq:(g + 1) * tq, :] = q_ref[:, hh * ATT_DH:(hh + 1) * ATT_DH].astype(BF16)
        m_ref[...] = jnp.full_like(m_ref, -3e38)

        def stage_a(kb, c):
            off = pl.multiple_of(kb * LANES, LANES)
            kblk = k_ref[pl.ds(off, LANES), n * ATT_DH:(n + 1) * ATT_DH].astype(BF16)
            mbt = mb_ref[:, pl.ds(off, LANES)]
            s = (_dot_nt(qs_ref[...], kblk) * ATT_SCALE + tab_ref[jnp.minimum(qi - kb, 2), n]
                 + jnp.concatenate([mbt] * ATT_GROUP, axis=0))
            s_ref[:, pl.ds(off, LANES)] = s
            m_ref[...] = jnp.maximum(m_ref[...], s)
            return c

        lax.fori_loop(0, nb, stage_a, 0)
        m_ref[...] = jnp.broadcast_to(jnp.max(m_ref[...], axis=-1, keepdims=True), m_ref.shape)
        l_ref[...] = jnp.zeros_like(l_ref)
        acc_ref[...] = jnp.zeros_like(acc_ref)

        def stage_b(kb, c):
            off = pl.multiple_of(kb * LANES, LANES)
            p = jnp.exp(s_ref[:, pl.ds(off, LANES)] - m_ref[...])
            l_ref[...] += p
            vblk = v_ref[pl.ds(off, LANES), n * ATT_DH:(n + 1) * ATT_DH].astype(BF16)
            acc_ref[...] += _dot(p.astype(BF16), vblk)
            return c

        lax.fori_loop(0, nb, stage_b, 0)
        o = acc_ref[...] / jnp.sum(l_ref[...], axis=-1, keepdims=True)
        for g in range(ATT_GROUP):
            hh = n * ATT_GROUP + g
            o_ref[:, hh * ATT_DH:(hh + 1) * ATT_DH] = o[g * tq:(g + 1) * tq]


def _dsa_prompt(x, w_in, kidx_g, kidx_b, t5_bias, B, T):
    M = B * T
    h_main, ik, ikw = _dsa_project(x, w_in, kidx_g, kidx_b, 1024 if M % 1024 == 0 else M)
    topk = min(TOPK_MAX, T // 4)
    tq = LANES
    assert _far_bucket_ok(LANES) and T % tq == 0
    nq = T // tq
    iqc = (Q_W + 2 * KV_W) // (IQ_W // 2)
    assert iqc * (IQ_W // 2) == Q_W + 2 * KV_W
    kcol = Q_W // KV_W
    gq = ATT_GROUP * tq
    o = pl.pallas_call(
        functools.partial(_dsap_kernel, topk=topk),
        grid=(B, nq),
        in_specs=[pl.BlockSpec((tq, IQ_W // 2), lambda b, qi: (b * nq + qi, iqc)),
                  pl.BlockSpec((tq, IQ_W // 2), lambda b, qi: (b * nq + qi, iqc + 1)),
                  pl.BlockSpec((tq, LANES), lambda b, qi: (b * nq + qi, 1)),
                  pl.BlockSpec((T, IDX_DH), lambda b, qi: (b, 0)),
                  pl.BlockSpec((tq, Q_W), lambda b, qi: (b * nq + qi, 0)),
                  pl.BlockSpec((T, KV_W), lambda b, qi: (b, kcol)),
                  pl.BlockSpec((T, KV_W), lambda b, qi: (b, kcol + 1)),
                  pl.BlockSpec((3, KV_HEADS, gq, LANES), lambda b, qi: (0, 0, 0, 0))],
        out_specs=pl.BlockSpec((tq, Q_W), lambda b, qi: (b * nq + qi, 0)),
        out_shape=jax.ShapeDtypeStruct((M, Q_W), F32),
        scratch_shapes=[pltpu.VMEM((IDX_HEADS * tq, IDX_DH), BF16),
                        pltpu.VMEM((IDX_HEADS * tq, LANES), F32),
                        pltpu.VMEM((tq, T), F32),
                        pltpu.VMEM((tq, T), I32),
                        pltpu.VMEM((tq, T), F32),
                        pltpu.VMEM((gq, ATT_DH), BF16),
                        pltpu.VMEM((gq, T), F32),
                        pltpu.VMEM((gq, LANES), F32),
                        pltpu.VMEM((gq, LANES), F32),
                        pltpu.VMEM((gq, ATT_DH), F32)],
        compiler_params=_cparams(("parallel", "arbitrary")),
    )(h_main, h_main, ikw, ik, h_main, h_main, h_main, _bias_table(t5_bias, tq, LANES))
    k_new = h_main[:, Q_W:Q_W + KV_W]
    v_new = h_main[:, Q_W + KV_W:Q_W + 2 * KV_W]
    return o, k_new, v_new, ik


PAGES_PER_STEP = 8


def _head_rows(ref, n):
    return ref[0, pl.ds(n, LANES, stride=KV_HEADS), :]


def _idxs_kernel(pt_ref, iq_ref, iw_ref, *refs, n_steps):
    G = PAGES_PER_STEP
    page_refs, kn_ref, o_ref = refs[:G], refs[G], refs[G + 1]
    step = pl.program_id(1)
    tp = o_ref.shape[0]
    iq = iq_ref[0].astype(BF16)
    iwb = iw_ref[0] * IDX_SCALE

    def scores(keys):
        r = jnp.maximum(_dot_nt(iq, keys.astype(BF16)), 0.0) * iwb
        acc = r[0:tp]
        for hh in range(1, IDX_HEADS):
            acc = acc + r[hh * tp:(hh + 1) * tp]
        return acc

    @pl.when(step < n_steps - 1)
    def _():
        for j in range(G):
            o_ref[:, j * LANES:(j + 1) * LANES] = scores(page_refs[j][0])

    @pl.when(step == n_steps - 1)
    def _():
        o_ref[...] = jnp.zeros_like(o_ref)
        o_ref[:, 0:LANES] = scores(kn_ref[0])


def _attns_kernel(pt_ref, q_ref, *refs, n_steps, n_pages):
    G = PAGES_PER_STEP
    kp, vp = refs[:G], refs[G:2 * G]
    kn_ref, vn_ref, mb_ref, tab_ref, o_ref, m_ref, l_ref, acc_ref = refs[2 * G:]
    step = pl.program_id(1)

    @pl.when(step == 0)
    def _():
        m_ref[...] = jnp.full_like(m_ref, -3e38)
        l_ref[...] = jnp.zeros_like(l_ref)
        acc_ref[...] = jnp.zeros_like(acc_ref)

    def process(pages):
        for n in range(KV_HEADS):
            qn = q_ref[0, n].astype(BF16)
            s_parts = []
            for (kr, vr, cls, cb) in pages:
                mbt = mb_ref[:, cb * LANES:(cb + 1) * LANES]
                s_parts.append(_dot_nt(qn, _head_rows(kr, n).astype(BF16)) * ATT_SCALE
                               + tab_ref[cls, n] + jnp.concatenate([mbt] * ATT_GROUP, axis=0))
            s = s_parts[0] if len(s_parts) == 1 else jnp.concatenate(s_parts, axis=1)
            m_old = m_ref[n]
            m_new = jnp.maximum(m_old, jnp.max(s, axis=-1, keepdims=True))
            a = jnp.exp(m_old - m_new)
            p = jnp.exp(s - m_new)
            l_ref[n] = a * l_ref[n] + jnp.sum(p, axis=-1, keepdims=True)
            pv = None
            for j, (kr, vr, cls, cb) in enumerate(pages):
                d = _dot(p[:, j * LANES:(j + 1) * LANES].astype(BF16), _head_rows(vr, n).astype(BF16))
                pv = d if pv is None else pv + d
            acc_ref[n] = a * acc_ref[n] + pv
            m_ref[n] = m_new

    @pl.when(step < n_steps - 1)
    def _():
        process([(kp[j], vp[j], jnp.minimum(n_pages - (step * G + j), 2), j) for j in range(G)])

    @pl.when(step == n_steps - 1)
    def _():
        process([(kn_ref, vn_ref, 0, 0)])
        for n in range(KV_HEADS):
            o_ref[0, n] = acc_ref[n] / l_ref[n]


def _dsa_sample(x, cache_k, cache_v, cache_kidx, page_table, w_in, kidx_g, kidx_b, t5_bias,
                B, T, Tp, past_len):
    M = B * Tp
    G = PAGES_PER_STEP
    page = cache_kidx.shape[1]
    n_pages = past_len // page
    assert page == LANES and _far_bucket_ok(page) and n_pages % G == 0
    h_main, ik, ikw = _dsa_project(x, w_in, kidx_g, kidx_b, M)
    topk = min(TOPK_MAX, (past_len + T) // 4)
    n_steps = n_pages // G + 1
    S = n_steps * G * page
    n_pool = cache_kidx.shape[0]
    ck = cache_k.reshape(n_pool, page * KV_HEADS, ATT_DH)
    cv = cache_v.reshape(n_pool, page * KV_HEADS, ATT_DH)
    pt = page_table.reshape(-1).astype(jnp.int32)
    k_new = h_main[:, Q_W:Q_W + KV_W]
    v_new = h_main[:, Q_W + KV_W:Q_W + 2 * KV_W]

    def as_page(a, w):
        return jnp.pad(a.reshape(B, Tp, w), ((0, 0), (0, page - Tp), (0, 0))).reshape(B, -1, ATT_DH)

    iq = h_main[:, Q_W + 2 * KV_W:].reshape(B, Tp, IDX_HEADS, IDX_DH)
    iq_st = jnp.transpose(iq, (0, 2, 1, 3)).reshape(B, IDX_HEADS * Tp, IDX_DH)
    iw = ikw[:, LANES:LANES + IDX_HEADS].reshape(B, Tp, IDX_HEADS)
    iw_st = jnp.broadcast_to(jnp.transpose(iw, (0, 2, 1)).reshape(B, IDX_HEADS * Tp, 1),
                             (B, IDX_HEADS * Tp, LANES))

    def page_spec(rows, j):
        return pl.BlockSpec(
            (1, rows, ATT_DH),
            lambda b, s, p: (p[b * n_pages + jnp.minimum(s * G + j, n_pages - 1)], 0, 0))

    stacked = pl.BlockSpec((1, IDX_HEADS * Tp, LANES), lambda b, s, p: (b, 0, 0))
    scores = pl.pallas_call(
        functools.partial(_idxs_kernel, n_steps=n_steps),
        grid_spec=pltpu.PrefetchScalarGridSpec(
            num_scalar_prefetch=1,
            grid=(B, n_steps),
            in_specs=[stacked, stacked] + [page_spec(page, j) for j in range(G)]
                     + [pl.BlockSpec((1, page, IDX_DH), lambda b, s, p: (b, 0, 0))],
            out_specs=pl.BlockSpec((Tp, G * page), lambda b, s, p: (b, s)),
        ),
        out_shape=jax.ShapeDtypeStruct((M, S), F32),
        compiler_params=_cparams(("parallel", "arbitrary")),
    )(pt, iq_st, iw_st, *([cache_kidx] * G), as_page(ik, IDX_DH))
    qpos = jnp.tile(past_len + jnp.arange(Tp, dtype=jnp.int32), B)[:, None]
    mbias = _select(scores, qpos, topk, M)

    gq = ATT_GROUP * Tp
    q = h_main[:, :Q_W].reshape(B, Tp, KV_HEADS, ATT_GROUP, ATT_DH)
    q_st = jnp.transpose(q, (0, 2, 3, 1, 4)).reshape(B, KV_HEADS, gq, ATT_DH)
    qspec = pl.BlockSpec((1, KV_HEADS, gq, ATT_DH), lambda b, s, p: (b, 0, 0, 0))
    newspec = pl.BlockSpec((1, page * KV_HEADS, ATT_DH), lambda b, s, p: (b, 0, 0))
    o_st = pl.pallas_call(
        functools.partial(_attns_kernel, n_steps=n_steps, n_pages=n_pages),
        grid_spec=pltpu.PrefetchScalarGridSpec(
            num_scalar_prefetch=1,
            grid=(B, n_steps),
            in_specs=[qspec] + [page_spec(page * KV_HEADS, j) for j in range(G)] * 2
                     + [newspec, newspec,
                        pl.BlockSpec((Tp, G * page), lambda b, s, p: (b, s)),
                        pl.BlockSpec((3, KV_HEADS, gq, page), lambda b, s, p: (0, 0, 0, 0))],
            out_specs=qspec,
            scratch_shapes=[pltpu.VMEM((KV_HEADS, gq, 1), F32),
                            pltpu.VMEM((KV_HEADS, gq, 1), F32),
                            pltpu.VMEM((KV_HEADS, gq, ATT_DH), F32)],
        ),
        out_shape=jax.ShapeDtypeStruct((B, KV_HEADS, gq, ATT_DH), F32),
        compiler_params=_cparams(("parallel", "arbitrary")),
    )(pt, q_st, *([ck] * G), *([cv] * G), as_page(k_new, KV_W), as_page(v_new, KV_W), mbias,
      _bias_table(t5_bias, Tp, page))
    o = jnp.transpose(o_st.reshape(B, KV_HEADS, ATT_GROUP, Tp, ATT_DH), (0, 3, 1, 2, 4)).reshape(M, Q_W)
    return o, k_new, v_new, ik


def _router_kernel(x_ref, w_ref, b_ref, e_ref, g_ref, p_ref, cnt_ref, carry_ref):
    i = pl.program_id(0)
    tm = x_ref.shape[0]
    E = N_EXPERTS

    @pl.when(i == 0)
    def _():
        carry_ref[...] = jnp.zeros_like(carry_ref)

    logits = _dot(x_ref[...].astype(BF16), w_ref[...].astype(BF16)) + b_ref[...]
    lane = lax.broadcasted_iota(I32, (tm, E), 1).astype(F32)
    lane_o = lax.broadcasted_iota(I32, (tm, LANES), 1)
    vals, idxs = [], []
    cur = logits
    onehot = jnp.zeros((tm, E), F32)
    for _ in range(TOP_K):
        mx = jnp.max(cur, axis=-1, keepdims=True)
        ix = jnp.min(jnp.where(cur == mx, lane, float(E)), axis=-1, keepdims=True)
        hit = lane == ix
        onehot = jnp.where(hit, 1.0, onehot)
        cur = jnp.where(hit, -jnp.inf, cur)
        vals.append(mx)
        idxs.append(ix)
    ex = [jnp.exp(v - vals[0]) for v in vals]
    den = ex[0] + ex[1] + ex[2] + ex[3]
    r_i = lax.broadcasted_iota(I32, (tm, tm), 0)
    c_i = lax.broadcasted_iota(I32, (tm, tm), 1)
    lower = jnp.where(c_i < r_i, 1.0, 0.0).astype(BF16)
    prefix = _dot(lower, onehot.astype(BF16)) + carry_ref[...]
    e_out = jnp.zeros((tm, LANES), I32)
    g_out = jnp.zeros((tm, LANES), F32)
    p_out = jnp.zeros((tm, LANES), I32)
    for k in range(TOP_K):
        pos = jnp.sum(jnp.where(lane == idxs[k], prefix, 0.0), axis=-1, keepdims=True)
        e_out = jnp.where(lane_o == k, idxs[k].astype(I32), e_out)
        g_out = jnp.where(lane_o == k, ex[k] / den, g_out)
        p_out = jnp.where(lane_o == k, pos.astype(I32), p_out)
    e_ref[...] = e_out
    g_ref[...] = g_out
    p_ref[...] = p_out
    carry_ref[...] = carry_ref[...] + jnp.sum(onehot, axis=0, keepdims=True)
    cnt_ref[...] = carry_ref[...].astype(I32)


def _router(x, w_router, b_router, layer, tm):
    T, D = x.shape
    E = N_EXPERTS
    outs = pl.pallas_call(
        _router_kernel,
        grid=(T // tm,),
        in_specs=[pl.BlockSpec((tm, D), lambda i: (i, 0)),
                  pl.BlockSpec((None, D, E), lambda i: (layer, 0, 0)),
                  pl.BlockSpec((None, 1, E), lambda i: (layer, 0, 0))],
        out_specs=[pl.BlockSpec((tm, LANES), lambda i: (i, 0)),
                   pl.BlockSpec((tm, LANES), lambda i: (i, 0)),
                   pl.BlockSpec((tm, LANES), lambda i: (i, 0)),
                   pl.BlockSpec((1, E), lambda i: (0, 0))],
        out_shape=[jax.ShapeDtypeStruct((T, LANES), I32),
                   jax.ShapeDtypeStruct((T, LANES), F32),
                   jax.ShapeDtypeStruct((T, LANES), I32),
                   jax.ShapeDtypeStruct((1, E), I32)],
        scratch_shapes=[pltpu.VMEM((1, E), F32)],
        compiler_params=_cparams(("arbitrary",)),
    )(x, w_router, b_router.reshape(-1, 1, E))
    return outs


def _row_copy(src, s_row, dst, d_row, sem):
    return pltpu.make_async_copy(src.at[pl.ds(s_row, 1)], dst.at[pl.ds(d_row, 1)], sem)


def _dispatch_kernel(nb_ref, tok_ref, x_hbm, o_ref, buf_ref, sem):
    blk = pl.program_id(0)
    tm = o_ref.shape[0]

    @pl.when(blk < nb_ref[0])
    def _():
        def issue(r, c):
            _row_copy(x_hbm, tok_ref[blk * tm + r], buf_ref, r, sem).start()
            return c

        lax.fori_loop(0, tm, issue, 0)
        pltpu.make_async_copy(x_hbm.at[pl.ds(0, tm)], buf_ref, sem).wait()
        o_ref[...] = buf_ref[...].astype(o_ref.dtype)

    @pl.when(blk >= nb_ref[0])
    def _():
        o_ref[...] = jnp.zeros_like(o_ref)


def _dispatch(x, row_tok, n_used, tm):
    T, D = x.shape
    n_rows = row_tok.shape[0]
    return pl.pallas_call(
        _dispatch_kernel,
        grid_spec=pltpu.PrefetchScalarGridSpec(
            num_scalar_prefetch=2,
            grid=(n_rows // tm,),
            in_specs=[pl.BlockSpec(memory_space=pl.ANY)],
            out_specs=pl.BlockSpec((tm, D), lambda i, nb, tok: (i, 0)),
            scratch_shapes=[pltpu.VMEM((tm, D), x.dtype), pltpu.SemaphoreType.DMA(())],
        ),
        out_shape=jax.ShapeDtypeStruct((n_rows, D), BF16),
        compiler_params=_cparams(("arbitrary",)),
    )(n_used, row_tok, x)


def _gu_kernel(be_ref, nb_ref, x_ref, wg_ref, wu_ref, bg_ref, bu_ref, o_ref, wgb_ref, wub_ref):
    rb = pl.program_id(1)
    prev = be_ref[jnp.maximum(rb - 1, 0)]
    changed = jnp.logical_or(rb == 0, be_ref[rb] != prev)

    @pl.when(jnp.logical_and(changed, rb < nb_ref[0]))
    def _():
        wgb_ref[...] = wg_ref[...].astype(BF16)
        wub_ref[...] = wu_ref[...].astype(BF16)

    @pl.when(rb < nb_ref[0])
    def _():
        xb = x_ref[...]
        gate = _dot(xb, wgb_ref[...]) + bg_ref[...]
        up = _dot(xb, wub_ref[...]) + bu_ref[...]
        gate = jnp.minimum(gate, SWIGLU_LIMIT)
        up = jnp.clip(up, -SWIGLU_LIMIT, SWIGLU_LIMIT)
        sig = 1.0 / (1.0 + jnp.exp(-(gate * SWIGLU_ALPHA)))
        o_ref[...] = ((up + 1.0) * (gate * sig)).astype(o_ref.dtype)

    @pl.when(rb >= nb_ref[0])
    def _():
        o_ref[...] = jnp.zeros_like(o_ref)


def _down_kernel(be_ref, nb_ref, h_ref, w_ref, b_ref, o_ref, wb_ref):
    rb = pl.program_id(1)
    prev = be_ref[jnp.maximum(rb - 1, 0)]
    changed = jnp.logical_or(rb == 0, be_ref[rb] != prev)

    @pl.when(jnp.logical_and(changed, rb < nb_ref[0]))
    def _():
        wb_ref[...] = w_ref[...].astype(BF16)

    @pl.when(rb < nb_ref[0])
    def _():
        o_ref[...] = _dot(h_ref[...], wb_ref[...]) + b_ref[...]

    @pl.when(rb >= nb_ref[0])
    def _():
        o_ref[...] = jnp.zeros_like(o_ref)


def _experts(xs, block_e, n_used, w_gu, b_gu, w_down, b_down, layer, tm, tn):
    n_rows, D = xs.shape
    DE = w_down.shape[2]
    NB = n_rows // tm
    ng = DE // tn
    rbc = lambda rb, nb: jnp.minimum(rb, nb[0] - 1)
    hid = pl.pallas_call(
        _gu_kernel,
        grid_spec=pltpu.PrefetchScalarGridSpec(
            num_scalar_prefetch=2,
            grid=(ng, NB),
            in_specs=[pl.BlockSpec((tm, D), lambda n, rb, be, nb: (rbc(rb, nb), 0)),
                      pl.BlockSpec((None, None, D, tn), lambda n, rb, be, nb: (layer, be[rbc(rb, nb)], 0, n)),
                      pl.BlockSpec((None, None, D, tn), lambda n, rb, be, nb: (layer, be[rbc(rb, nb)], 0, ng + n)),
                      pl.BlockSpec((None, None, 1, tn), lambda n, rb, be, nb: (layer, be[rbc(rb, nb)], 0, n)),
                      pl.BlockSpec((None, None, 1, tn), lambda n, rb, be, nb: (layer, be[rbc(rb, nb)], 0, ng + n))],
            out_specs=pl.BlockSpec((tm, tn), lambda n, rb, be, nb: (rb, n)),
            scratch_shapes=[pltpu.VMEM((D, tn), BF16), pltpu.VMEM((D, tn), BF16)],
        ),
        out_shape=jax.ShapeDtypeStruct((n_rows, DE), BF16),
        compiler_params=_cparams(("arbitrary", "arbitrary")),
    )(block_e, n_used, xs, w_gu, w_gu, b_gu.reshape(DEPTH, N_EXPERTS, 1, 2 * DE),
      b_gu.reshape(DEPTH, N_EXPERTS, 1, 2 * DE))
    nd = D // tn
    out = pl.pallas_call(
        _down_kernel,
        grid_spec=pltpu.PrefetchScalarGridSpec(
            num_scalar_prefetch=2,
            grid=(nd, NB),
            in_specs=[pl.BlockSpec((tm, DE), lambda n, rb, be, nb: (rbc(rb, nb), 0)),
                      pl.BlockSpec((None, None, DE, tn), lambda n, rb, be, nb: (layer, be[rbc(rb, nb)], 0, n)),
                      pl.BlockSpec((None, None, 1, tn), lambda n, rb, be, nb: (layer, be[rbc(rb, nb)], 0, n))],
            out_specs=pl.BlockSpec((tm, tn), lambda n, rb, be, nb: (rb, n)),
            scratch_shapes=[pltpu.VMEM((DE, tn), BF16)],
        ),
        out_shape=jax.ShapeDtypeStruct((n_rows, D), F32),
        compiler_params=_cparams(("arbitrary", "arbitrary")),
    )(block_e, n_used, hid, w_down, b_down.reshape(DEPTH, N_EXPERTS, 1, D))
    return out


def _combine_kernel(dest_ref, g_ref, x_ref, rows_hbm, lg_ref, lb_ref, o_ref, buf_ref, sem, *, tb):
    def issue(r, c):
        for k in range(TOP_K):
            _row_copy(rows_hbm, dest_ref[r * TOP_K + k], buf_ref.at[k], r, sem).start()
        return c

    lax.fori_loop(0, tb, issue, 0)
    for k in range(TOP_K):
        pltpu.make_async_copy(rows_hbm.at[pl.ds(0, tb)], buf_ref.at[k], sem).wait()
    g = g_ref[...]
    y = buf_ref[0] * g[:, 0:1]
    for k in range(1, TOP_K):
        y = y + buf_ref[k] * g[:, k:k + 1]
    o_ref[...] = _ln_rows(DEEPNORM_ALPHA * x_ref[...] + y, lg_ref[...], lb_ref[...])


def _combine(x, rows, dest_flat, gates, ln_g, ln_b, tb):
    T, D = x.shape
    return pl.pallas_call(
        functools.partial(_combine_kernel, tb=tb),
        grid=(T // tb,),
        in_specs=[pl.BlockSpec((tb * TOP_K,), lambda i: (i,), memory_space=pltpu.SMEM),
                  pl.BlockSpec((tb, LANES), lambda i: (i, 0)),
                  pl.BlockSpec((tb, D), lambda i: (i, 0)),
                  pl.BlockSpec(memory_space=pl.ANY),
                  pl.BlockSpec((1, D), lambda i: (0, 0)),
                  pl.BlockSpec((1, D), lambda i: (0, 0))],
        out_specs=pl.BlockSpec((tb, D), lambda i: (i, 0)),
        out_shape=jax.ShapeDtypeStruct((T, D), F32),
        scratch_shapes=[pltpu.VMEM((TOP_K, tb, D), F32), pltpu.SemaphoreType.DMA(())],
        compiler_params=_cparams(("arbitrary",)),
    )(dest_flat, gates, x, rows, ln_g.reshape(1, D), ln_b.reshape(1, D))


def _moe_ln(x, w_router, b_router, w_gu, b_gu, w_down, b_down, ln_g, ln_b, layer, tm_r, tm, tb):
    T, D = x.shape
    E = N_EXPERTS
    top_e, gates, pos, counts = _router(x, w_router, b_router, layer, tm_r)
    counts = counts[0]
    padded = ((counts + tm - 1) // tm) * tm
    pend = jnp.cumsum(padded)
    pstart = pend - padded
    e4 = top_e[:, :TOP_K]
    dest = (pstart[e4] + pos[:, :TOP_K]).reshape(-1).astype(jnp.int32)
    nb_max = (T * TOP_K) // tm + E
    block_e = jnp.minimum(jnp.searchsorted(pend, jnp.arange(nb_max, dtype=jnp.int32) * tm, side='right'),
                          E - 1).astype(jnp.int32)
    n_used = (pend[-1] // tm).astype(jnp.int32).reshape(1)
    row_tok = jnp.zeros((nb_max * tm,), jnp.int32).at[dest].set(
        jnp.arange(T * TOP_K, dtype=jnp.int32) // TOP_K)
    xs = _dispatch(x, row_tok, n_used, tm)
    rows = _experts(xs, block_e, n_used, w_gu, b_gu, w_down, b_down, layer, tm, 512)
    return _combine(x, rows, dest, gates, ln_g, ln_b, tb)


def kernel(x_prompt, x_sample, state_ret, cache_k, cache_v, cache_kidx, page_table, t5_bias, ret_w_in, ret_gn_g, ret_gn_b, ret_w_o, dsa_w_in, dsa_kidx_g, dsa_kidx_b, dsa_w_o, ln_mix_g, ln_mix_b, ln_ffn_g, ln_ffn_b, moe_w_router, moe_b_router, moe_w_gu, moe_b_gu, moe_w_down, moe_b_down):
    B, T, D = x_prompt.shape
    Bs, Ts, _ = x_sample.shape
    Tsp = SUBLANES
    past_len = page_table.shape[1] * cache_k.shape[2]
    Mp = B * T
    Ms = Bs * Tsp

    def pad_s(a):
        return jnp.pad(a.reshape(Bs, Ts, -1), ((0, 0), (0, Tsp - Ts), (0, 0))).reshape(Ms, -1)

    def unpad_s(a):
        return a.reshape(Bs, Tsp, -1)[:, :Ts].reshape(Bs * Ts, -1)

    xp = x_prompt.reshape(Mp, D)
    xs = x_sample.reshape(Bs * Ts, D)
    tm_p = 1024 if Mp % 1024 == 0 else Mp
    C = RET_CHUNK if T % RET_CHUNK == 0 else T

    moe = lambda x, i, tm_r, tm, tb: _moe_ln(
        x, moe_w_router, moe_b_router, moe_w_gu, moe_b_gu, moe_w_down, moe_b_down,
        ln_ffn_g[i], ln_ffn_b[i], i, tm_r, tm, tb)

    w_in, w_o = ret_w_in[0], ret_w_o[0]
    hp = _matmul(xp, w_in, 0, w_in.shape[1], tm_p, 512)
    op, ret_p = _retention(hp, jnp.zeros((B,) + state_ret.shape[2:], F32), 0, B, T, C, C,
                           ret_gn_g[0], ret_gn_b[0])
    mp = _matmul(op, w_o, 0, D, 512, 512)
    xp = _res_ln(xp, mp, ln_mix_g[0], ln_mix_b[0], 256)

    xs_pad = pad_s(xs)
    hs = _matmul(xs_pad, w_in, 0, w_in.shape[1], Ms, 512)
    os_, ret_s = _retention(hs, state_ret[0], past_len, Bs, Ts, Ts, Tsp, ret_gn_g[0], ret_gn_b[0])
    ms = _matmul(os_, w_o, 0, D, Ms, 512)
    xs = unpad_s(_res_ln(xs_pad, ms, ln_mix_g[0], ln_mix_b[0], Ms))

    xp = moe(xp, 0, 256, 256, 256)
    xs = moe(xs, 0, Bs * Ts, 16, Bs * Ts)

    w_in, w_o = dsa_w_in[0], dsa_w_o[0]
    ap, k_p, v_p, ik_p = _dsa_prompt(xp, w_in, dsa_kidx_g[0], dsa_kidx_b[0], t5_bias, B, T)
    mp = _matmul(ap, w_o, 0, D, tm_p, 512)
    xp = _res_ln(xp, mp, ln_mix_g[1], ln_mix_b[1], 256)

    xs_pad = pad_s(xs)
    as_, k_s, v_s, ik_s = _dsa_sample(xs_pad, cache_k[0], cache_v[0], cache_kidx[0], page_table, w_in,
                                      dsa_kidx_g[0], dsa_kidx_b[0], t5_bias, Bs, Ts, Tsp, past_len)
    ms = _matmul(as_, w_o, 0, D, Ms, 512)
    xs = unpad_s(_res_ln(xs_pad, ms, ln_mix_g[1], ln_mix_b[1], Ms))

    xp = moe(xp, 1, 256, 256, 256)
    xs = moe(xs, 1, Bs * Ts, 16, Bs * Ts)

    return (xp.reshape(B, T, D), xs.reshape(Bs, Ts, D),
            ret_p[None], ret_s[None],
            k_p.reshape(1, B, T, KV_HEADS, ATT_DH), v_p.reshape(1, B, T, KV_HEADS, ATT_DH),
            ik_p.reshape(1, B, T, IDX_DH),
            unpad_s(k_s).reshape(1, Bs, Ts, KV_HEADS, ATT_DH),
            unpad_s(v_s).reshape(1, Bs, Ts, KV_HEADS, ATT_DH),
            unpad_s(ik_s).reshape(1, Bs, Ts, IDX_DH))
```

```python
import functools
import math

import numpy as np
import jax
import jax.numpy as jnp
from jax import lax
from jax.experimental import pallas as pl
from jax.experimental.pallas import tpu as pltpu

F32 = jnp.float32
BF16 = jnp.bfloat16
I32 = jnp.int32

RET_HEADS = 8
RET_HB = 4
RET_CHUNK = 128
ROPE_BASE = 10000.0
ATT_HEADS = 16
ATT_DH = 128
KV_HEADS = 4
ATT_GROUP = ATT_HEADS // KV_HEADS
IDX_HEADS = 16
IDX_DH = 128
TOPK_MAX = 256
T5_BUCKETS = 32
T5_MAX_DIST = 128
N_EXPERTS = 32
TOP_K = 4
SWIGLU_LIMIT = 7.0
SWIGLU_ALPHA = 1.702
DEPTH = 2
DEEPNORM_ALPHA = (2.0 * DEPTH) ** 0.25
LN_EPS = 1e-5

LANES = 128
SUBLANES = 8
VMEM_LIMIT = 56 * 1024 * 1024
NEG_BIG = -1e30
INT_MIN = -(2 ** 31)


def _cparams(sem):
    return pltpu.CompilerParams(dimension_semantics=sem, vmem_limit_bytes=VMEM_LIMIT)


def _ln_rows(x, g, b):
    mu = jnp.mean(x, axis=-1, keepdims=True)
    xc = x - mu
    var = jnp.mean(xc * xc, axis=-1, keepdims=True)
    return xc * lax.rsqrt(var + LN_EPS) * g + b


def _dot(a, b):
    return jnp.dot(a, b, preferred_element_type=F32)


def _dot_nt(a, b):
    return lax.dot_general(a, b, (((1,), (1,)), ((), ())), preferred_element_type=F32)


def _mm_kernel(x_ref, w_ref, o_ref, xb_ref):
    @pl.when(pl.program_id(1) == 0)
    def _():
        xb_ref[...] = x_ref[...].astype(BF16)

    o_ref[...] = _dot(xb_ref[...], w_ref[...].astype(BF16)).astype(o_ref.dtype)


def _matmul(x, w, col0, ncols, tm, tn):
    M, K = x.shape
    assert M % tm == 0 and col0 % tn == 0 and ncols % tn == 0
    c0 = col0 // tn
    return pl.pallas_call(
        _mm_kernel,
        grid=(M // tm, ncols // tn),
        in_specs=[pl.BlockSpec((tm, K), lambda i, j: (i, 0)),
                  pl.BlockSpec((K, tn), lambda i, j: (0, j + c0))],
        out_specs=pl.BlockSpec((tm, tn), lambda i, j: (i, j)),
        out_shape=jax.ShapeDtypeStruct((M, ncols), F32),
        scratch_shapes=[pltpu.VMEM((tm, K), BF16)],
        compiler_params=_cparams(("parallel", "arbitrary")),
    )(x, w)


def _res_ln_kernel(x_ref, h_ref, g_ref, b_ref, o_ref, *, alpha):
    o_ref[...] = _ln_rows(alpha * x_ref[...] + h_ref[...], g_ref[...], b_ref[...])


def _ln_kernel(x_ref, g_ref, b_ref, o_ref):
    o_ref[...] = _ln_rows(x_ref[...], g_ref[...], b_ref[...])


def _res_ln(x, h, g, b, tm):
    M, D = x.shape
    row = pl.BlockSpec((tm, D), lambda i: (i, 0))
    par = pl.BlockSpec((1, D), lambda i: (0, 0))
    return pl.pallas_call(
        functools.partial(_res_ln_kernel, alpha=DEEPNORM_ALPHA),
        grid=(M // tm,),
        in_specs=[row, row, par, par],
        out_specs=row,
        out_shape=jax.ShapeDtypeStruct((M, D), F32),
        compiler_params=_cparams(("parallel",)),
    )(x, h, g.reshape(1, D), b.reshape(1, D))


def _ln_cols(x, col_blk, g, b, tm):
    M = x.shape[0]
    D = LANES
    par = pl.BlockSpec((1, D), lambda i: (0, 0))
    return pl.pallas_call(
        _ln_kernel,
        grid=(M // tm,),
        in_specs=[pl.BlockSpec((tm, D), lambda i: (i, col_blk)), par, par],
        out_specs=pl.BlockSpec((tm, D), lambda i: (i, 0)),
        out_shape=jax.ShapeDtypeStruct((M, D), F32),
        compiler_params=_cparams(("parallel",)),
    )(x, g.reshape(1, D), b.reshape(1, D))


def _ret_kernel(q_ref, k_ref, v_ref, g_ref, cos_ref, sin_ref, intra_ref, qd_ref, kd_ref, sd_ref,
                s0_ref, gng_ref, gnb_ref, o_ref, so_ref, st_ref, *, dk):
    c = pl.program_id(2)
    half = dk // 2
    dv = v_ref.shape[1] // RET_HB

    @pl.when(c == 0)
    def _():
        st_ref[...] = s0_ref[0]

    cos = cos_ref[...]
    sin = sin_ref[...]

    def rot(x):
        x1, x2 = x[:, :half], x[:, half:]
        return jnp.concatenate([x1 * cos - x2 * sin, x1 * sin + x2 * cos], axis=-1)

    for j in range(RET_HB):
        q = rot(q_ref[:, j * dk:(j + 1) * dk])
        k = rot(k_ref[:, j * dk:(j + 1) * dk]) * (dk ** -0.5)
        vb = v_ref[:, j * dv:(j + 1) * dv].astype(BF16)
        qb = q.astype(BF16)
        st = st_ref[j]
        scores = _dot_nt(qb, k.astype(BF16)) * intra_ref[j]
        o = _dot(scores.astype(BF16), vb)
        o = o + _dot(qb, st.astype(BF16)) * qd_ref[j]
        kd = (k * kd_ref[j]).astype(BF16)
        new_st = st * sd_ref[j] + lax.dot_general(kd, vb, (((0,), (0,)), ((), ())),
                                                  preferred_element_type=F32)
        st_ref[j] = new_st
        so_ref[0, j] = new_st
        y = _ln_rows(o, gng_ref[j], gnb_ref[j])
        g = g_ref[:, j * dv:(j + 1) * dv]
        o_ref[:, j * dv:(j + 1) * dv] = y * (g * (1.0 / (1.0 + jnp.exp(-g))))


def _ret_log_decay():
    return jnp.log1p(-jnp.exp2(-5.0 - jnp.arange(RET_HEADS, dtype=F32)))


def _retention(h, state0, pos0, B, T, C, Cp, gn_g, gn_b):
    H = RET_HEADS
    DK, DV = state0.shape[2], state0.shape[3]
    NC = T // C
    Tp = NC * Cp
    half = DK // 2
    log_g = _ret_log_decay()
    pos = (pos0 + jnp.arange(Tp, dtype=jnp.int32)).astype(F32)
    inv = ROPE_BASE ** (-jnp.arange(half, dtype=F32) / half)
    ang = pos[:, None] * inv[None, :]
    cos, sin = jnp.cos(ang), jnp.sin(ang)
    p = jnp.arange(C, dtype=F32)
    rel = p[:, None] - p[None, :]
    intra = jnp.where(rel[None] >= 0, jnp.exp(log_g[:, None, None] * jnp.maximum(rel, 0.0)[None]), 0.0)
    qd = jnp.exp((p[:, None] + 1.0) * log_g[None, :]).T[:, :, None]
    kd = jnp.exp((C - 1.0 - p)[:, None] * log_g[None, :]).T[:, :, None]
    sd = jnp.exp(C * log_g)[:, None, None]
    padc = Cp - C
    intra = jnp.pad(intra, ((0, 0), (0, padc), (0, padc)))
    qd = jnp.pad(qd, ((0, 0), (0, padc), (0, 0)))
    kd = jnp.pad(kd, ((0, 0), (0, padc), (0, 0)))

    HB = RET_HB
    assert H % HB == 0
    kb = H // HB
    vb = (2 * H * DK) // (HB * DV)
    gb = vb + H // HB
    assert vb * HB * DV == 2 * H * DK
    row = lambda b, hg, c: b * NC + c
    in_specs = [
        pl.BlockSpec((Cp, HB * DK), lambda b, hg, c: (row(b, hg, c), hg)),
        pl.BlockSpec((Cp, HB * DK), lambda b, hg, c: (row(b, hg, c), kb + hg)),
        pl.BlockSpec((Cp, HB * DV), lambda b, hg, c: (row(b, hg, c), vb + hg)),
        pl.BlockSpec((Cp, HB * DV), lambda b, hg, c: (row(b, hg, c), gb + hg)),
        pl.BlockSpec((Cp, half), lambda b, hg, c: (c, 0)),
        pl.BlockSpec((Cp, half), lambda b, hg, c: (c, 0)),
        pl.BlockSpec((HB, Cp, Cp), lambda b, hg, c: (hg, 0, 0)),
        pl.BlockSpec((HB, Cp, 1), lambda b, hg, c: (hg, 0, 0)),
        pl.BlockSpec((HB, Cp, 1), lambda b, hg, c: (hg, 0, 0)),
        pl.BlockSpec((HB, 1, 1), lambda b, hg, c: (hg, 0, 0)),
        pl.BlockSpec((1, HB, DK, DV), lambda b, hg, c: (b, hg, 0, 0)),
        pl.BlockSpec((HB, 1, DV), lambda b, hg, c: (hg, 0, 0)),
        pl.BlockSpec((HB, 1, DV), lambda b, hg, c: (hg, 0, 0)),
    ]
    out_specs = [
        pl.BlockSpec((Cp, HB * DV), lambda b, hg, c: (row(b, hg, c), hg)),
        pl.BlockSpec((1, HB, DK, DV), lambda b, hg, c: (b, hg, 0, 0)),
    ]
    o, st = pl.pallas_call(
        functools.partial(_ret_kernel, dk=DK),
        grid=(B, H // HB, NC),
        in_specs=in_specs,
        out_specs=out_specs,
        out_shape=[jax.ShapeDtypeStruct((B * Tp, H * DV), F32),
                   jax.ShapeDtypeStruct((B, H, DK, DV), F32)],
        scratch_shapes=[pltpu.VMEM((HB, DK, DV), F32)],
        compiler_params=_cparams(("parallel", "parallel", "arbitrary")),
    )(h, h, h, h, cos, sin, intra, qd, kd, sd, state0,
      gn_g.reshape(H, 1, DV), gn_b.reshape(H, 1, DV))
    return o, st


IDX_SCALE = IDX_HEADS ** -0.5 * IDX_DH ** -0.5


KEY_SB = 4 * LANES


def _select_rows(sc, qpos, topk, key_ref, o_ref, nsb=None):
    tr, S = sc.shape
    col = lax.broadcasted_iota(I32, (tr, S), 1)
    valid = col <= qpos
    bits = pltpu.bitcast(sc, I32)
    key = jnp.where(sc == 0.0, 0, bits ^ ((bits >> 31) & 0x7FFFFFFF))
    key_ref[...] = jnp.where(valid, key, INT_MIN)

    if nsb is None:
        nblk = S // LANES

        def count(pred):
            return jnp.sum(jnp.where(pred(key_ref[...]), 1.0, 0.0), axis=-1, keepdims=True)
    else:
        nblk = nsb * (KEY_SB // LANES)

        def count(pred):
            def sb_step(sb, acc):
                off = pl.multiple_of(sb * KEY_SB, KEY_SB)
                w = jnp.where(pred(key_ref[:, pl.ds(off, KEY_SB)]), 1.0, 0.0)
                for j in range(KEY_SB // LANES):
                    acc = acc + w[:, j * LANES:(j + 1) * LANES]
                return acc

            acc = lax.fori_loop(0, nsb, sb_step, jnp.zeros((tr, LANES), F32))
            return jnp.sum(acc, axis=-1, keepdims=True)

    def bit_step(i, ans):
        cand = ans | jnp.left_shift(jnp.int32(1), 31 - i)
        cand_s = cand ^ INT_MIN
        return jnp.where(count(lambda k: k >= cand_s) >= topk, cand, ans)

    ans = lax.fori_loop(0, 32, bit_step, jnp.zeros((tr, 1), I32))
    thr = ans ^ INT_MIN
    need = topk - count(lambda k: k > thr)

    r_i = lax.broadcasted_iota(I32, (LANES, LANES), 0)
    c_i = lax.broadcasted_iota(I32, (LANES, LANES), 1)
    upper = jnp.where(r_i < c_i, 1.0, 0.0).astype(BF16)

    def blk_step(j, carry):
        off = pl.multiple_of(j * LANES, LANES)
        kb = key_ref[:, pl.ds(off, LANES)]
        eq = jnp.where(kb == thr, jnp.where(kb == INT_MIN, 0.0, 1.0), 0.0)
        pre = _dot(eq.astype(BF16), upper) + carry
        take = jnp.where(pre < need, eq, 0.0)
        sel = jnp.where(kb > thr, 1.0, take)
        o_ref[:, pl.ds(off, LANES)] = jnp.where(sel > 0.5, 0.0, NEG_BIG)
        return carry + jnp.sum(eq, axis=-1, keepdims=True)

    lax.fori_loop(0, nblk, blk_step, jnp.zeros((tr, 1), F32))


def _select_kernel(sc_ref, qpos_ref, o_ref, key_ref, *, topk):
    _select_rows(sc_ref[...], qpos_ref[...], topk, key_ref, o_ref)


def _select(scores, qpos, topk, tr):
    R, S = scores.shape
    return pl.pallas_call(
        functools.partial(_select_kernel, topk=topk),
        grid=(R // tr,),
        in_specs=[pl.BlockSpec((tr, S), lambda i: (i, 0)),
                  pl.BlockSpec((tr, 1), lambda i: (i, 0))],
        out_specs=pl.BlockSpec((tr, S), lambda i: (i, 0)),
        out_shape=jax.ShapeDtypeStruct((R, S), F32),
        scratch_shapes=[pltpu.VMEM((tr, S), I32)],
        compiler_params=_cparams(("parallel",)),
    )(scores, qpos)


def _t5_bucket(dist):
    max_exact = T5_BUCKETS // 2
    is_small = dist < max_exact
    d = jnp.maximum(dist, 1).astype(F32)
    large = max_exact + (jnp.log(d / max_exact) / math.log(T5_MAX_DIST / max_exact)
                         * (T5_BUCKETS - max_exact)).astype(jnp.int32)
    large = jnp.minimum(large, T5_BUCKETS - 1)
    return jnp.where(is_small, dist, large)


def _bias_table(t5_bias, tq, tk):
    i = jnp.arange(tq, dtype=jnp.int32)[:, None]
    j = jnp.arange(tk, dtype=jnp.int32)[None, :]
    tabs = []
    for c in range(3):
        dist = jnp.maximum(c * tk + i - j, 0)
        tabs.append(jnp.transpose(t5_bias[_t5_bucket(dist)], (2, 0, 1)))
    return jnp.stack(tabs).astype(F32).reshape(3, KV_HEADS, ATT_GROUP * tq, tk)


def _far_bucket_ok(tk):
    d = np.arange(tk + 1, 8 * tk + 2).astype(np.float32)
    me = T5_BUCKETS // 2
    large = me + (np.log(d / me) / math.log(T5_MAX_DIST / me) * (T5_BUCKETS - me)).astype(np.int32)
    return bool(np.all(np.minimum(large, T5_BUCKETS - 1) == T5_BUCKETS - 1))


Q_W = ATT_HEADS * ATT_DH
KV_W = KV_HEADS * ATT_DH
IQ_W = IDX_HEADS * IDX_DH
DSA_MAIN = Q_W + 2 * KV_W + IQ_W
ATT_SCALE = ATT_DH ** -0.5


def _dsa_project(x, w_in, kidx_g, kidx_b, tm):
    h_main = _matmul(x, w_in, 0, DSA_MAIN, tm, 512)
    tail = w_in.shape[1] - DSA_MAIN
    w_tail = jnp.pad(w_in[:, DSA_MAIN:], ((0, 0), (0, 2 * LANES - tail)))
    ikw = _matmul(x, w_tail, 0, 2 * LANES, tm, 2 * LANES)
    ik = _ln_cols(ikw, 0, kidx_g, kidx_b, min(tm, 512))
    return h_main, ik, ikw


def _dsap_kernel(iq0_ref, iq1_ref, iw_ref, ik_ref, q_ref, k_ref, v_ref, tab_ref, o_ref,
                 iqs_ref, iwb_ref, sc_ref, key_ref, mb_ref, qs_ref, s_ref, m_ref, l_ref, acc_ref,
                 *, topk):
    qi = pl.program_id(1)
    tq = q_ref.shape[0]
    half = IDX_HEADS // 2
    nj = KEY_SB // LANES
    nsb = (qi + nj) // nj

    iw = iw_ref[...]
    for hh in range(IDX_HEADS):
        src = iq0_ref if hh < half else iq1_ref
        j = hh % half
        iqs_ref[hh * tq:(hh + 1) * tq, :] = src[:, j * IDX_DH:(j + 1) * IDX_DH].astype(BF16)
        iwb_ref[hh * tq:(hh + 1) * tq, :] = jnp.broadcast_to(iw[:, hh:hh + 1] * IDX_SCALE, (tq, LANES))

    sc_ref[...] = jnp.zeros_like(sc_ref)

    def sc_sb(sb, c):
        for j in range(nj):
            off = pl.multiple_of(sb * KEY_SB + j * LANES, LANES)
            ikb = ik_ref[pl.ds(off, LANES), :].astype(BF16)
            r = jnp.maximum(_dot_nt(iqs_ref[...], ikb), 0.0) * iwb_ref[...]
            acc = r[0:tq]
            for hh in range(1, IDX_HEADS):
                acc = acc + r[hh * tq:(hh + 1) * tq]
            sc_ref[:, pl.ds(off, LANES)] = acc
        return c

    lax.fori_loop(0, nsb, sc_sb, 0)

    qpos = qi * tq + lax.broadcasted_iota(I32, (tq, 1), 0)
    _select_rows(sc_ref[...], qpos, topk, key_ref, mb_ref, nsb)

    for n in range(KV_HEADS):
        for g in range(ATT_GROUP):
            hh = n * ATT_GROUP + g
            qs_ref[g * tq:(g + 1) * tq, :] = q_ref[:, hh * ATT_DH:(hh + 1) * ATT_DH].astype(BF16)
        m_ref[...] = jnp.full_like(m_ref, -3e38)

        def stage_a(sb, c):
            off = pl.multiple_of(sb * KEY_SB, KEY_SB)
            kblk = k_ref[pl.ds(off, KEY_SB), n * ATT_DH:(n + 1) * ATT_DH].astype(BF16)
            s = _dot_nt(qs_ref[...], kblk) * ATT_SCALE
            m = m_ref[...]
            for j in range(nj):
                offj = pl.multiple_of(sb * KEY_SB + j * LANES, LANES)
                mbt = mb_ref[:, pl.ds(offj, LANES)]
                cls = jnp.clip(qi - (sb * nj + j), 0, 2)
                sj = (s[:, j * LANES:(j + 1) * LANES] + tab_ref[cls, n]
                      + jnp.concatenate([mbt] * ATT_GROUP, axis=0))
                s_ref[:, pl.ds(offj, LANES)] = sj
                m = jnp.maximum(m, sj)
            m_ref[...] = m
            return c

        lax.fori_loop(0, nsb, stage_a, 0)
        m_ref[...] = jnp.broadcast_to(jnp.max(m_ref[...], axis=-1, keepdims=True), m_ref.shape)
        l_ref[...] = jnp.zeros_like(l_ref)
        acc_ref[...] = jnp.zeros_like(acc_ref)

        def stage_b(sb, c):
            off = pl.multiple_of(sb * KEY_SB, KEY_SB)
            m = m_ref[...]
            l = l_ref[...]
            ps = []
            for j in range(nj):
                offj = pl.multiple_of(sb * KEY_SB + j * LANES, LANES)
                pj = jnp.exp(s_ref[:, pl.ds(offj, LANES)] - m)
                l = l + pj
                ps.append(pj.astype(BF16))
            l_ref[...] = l
            vblk = v_ref[pl.ds(off, KEY_SB), n * ATT_DH:(n + 1) * ATT_DH].astype(BF16)
            acc_ref[...] += _dot(jnp.concatenate(ps, axis=1), vblk)
            return c

        lax.fori_loop(0, nsb, stage_b, 0)
        o = acc_ref[...] / jnp.sum(l_ref[...], axis=-1, keepdims=True)
        for g in range(ATT_GROUP):
            hh = n * ATT_GROUP + g
            o_ref[:, hh * ATT_DH:(hh + 1) * ATT_DH] = o[g * tq:(g + 1) * tq]


def _dsa_prompt(x, w_in, kidx_g, kidx_b, t5_bias, B, T):
    M = B * T
    h_main, ik, ikw = _dsa_project(x, w_in, kidx_g, kidx_b, 1024 if M % 1024 == 0 else M)
    topk = min(TOPK_MAX, T // 4)
    tq = LANES
    assert _far_bucket_ok(LANES) and T % KEY_SB == 0
    nq = T // tq
    iqc = (Q_W + 2 * KV_W) // (IQ_W // 2)
    assert iqc * (IQ_W // 2) == Q_W + 2 * KV_W
    kcol = Q_W // KV_W
    gq = ATT_GROUP * tq
    o = pl.pallas_call(
        functools.partial(_dsap_kernel, topk=topk),
        grid=(B, nq),
        in_specs=[pl.BlockSpec((tq, IQ_W // 2), lambda b, qi: (b * nq + qi, iqc)),
                  pl.BlockSpec((tq, IQ_W // 2), lambda b, qi: (b * nq + qi, iqc + 1)),
                  pl.BlockSpec((tq, LANES), lambda b, qi: (b * nq + qi, 1)),
                  pl.BlockSpec((T, IDX_DH), lambda b, qi: (b, 0)),
                  pl.BlockSpec((tq, Q_W), lambda b, qi: (b * nq + qi, 0)),
                  pl.BlockSpec((T, KV_W), lambda b, qi: (b, kcol)),
                  pl.BlockSpec((T, KV_W), lambda b, qi: (b, kcol + 1)),
                  pl.BlockSpec((3, KV_HEADS, gq, LANES), lambda b, qi: (0, 0, 0, 0))],
        out_specs=pl.BlockSpec((tq, Q_W), lambda b, qi: (b * nq + qi, 0)),
        out_shape=jax.ShapeDtypeStruct((M, Q_W), F32),
        scratch_shapes=[pltpu.VMEM((IDX_HEADS * tq, IDX_DH), BF16),
                        pltpu.VMEM((IDX_HEADS * tq, LANES), F32),
                        pltpu.VMEM((tq, T), F32),
                        pltpu.VMEM((tq, T), I32),
                        pltpu.VMEM((tq, T), F32),
                        pltpu.VMEM((gq, ATT_DH), BF16),
                        pltpu.VMEM((gq, T), F32),
                        pltpu.VMEM((gq, LANES), F32),
                        pltpu.VMEM((gq, LANES), F32),
                        pltpu.VMEM((gq, ATT_DH), F32)],
        compiler_params=_cparams(("parallel", "arbitrary")),
    )(h_main, h_main, ikw, ik, h_main, h_main, h_main, _bias_table(t5_bias, tq, LANES))
    k_new = h_main[:, Q_W:Q_W + KV_W]
    v_new = h_main[:, Q_W + KV_W:Q_W + 2 * KV_W]
    return o, k_new, v_new, ik


PAGES_PER_STEP = 8


def _head_rows(ref, n):
    return ref[0, pl.ds(n, LANES, stride=KV_HEADS), :]


def _idxs_kernel(pt_ref, iq_ref, iw_ref, *refs, n_steps):
    G = PAGES_PER_STEP
    page_refs, kn_ref, o_ref = refs[:G], refs[G], refs[G + 1]
    step = pl.program_id(1)
    tp = o_ref.shape[0]
    iq = iq_ref[0].astype(BF16)
    iwb = iw_ref[0] * IDX_SCALE

    def scores(keys):
        r = jnp.maximum(_dot_nt(iq, keys.astype(BF16)), 0.0) * iwb
        acc = r[0:tp]
        for hh in range(1, IDX_HEADS):
            acc = acc + r[hh * tp:(hh + 1) * tp]
        return acc

    @pl.when(step < n_steps - 1)
    def _():
        for j in range(G):
            o_ref[:, j * LANES:(j + 1) * LANES] = scores(page_refs[j][0])

    @pl.when(step == n_steps - 1)
    def _():
        o_ref[...] = jnp.zeros_like(o_ref)
        o_ref[:, 0:LANES] = scores(kn_ref[0])


def _attns_kernel(pt_ref, q_ref, *refs, n_steps, n_pages):
    G = PAGES_PER_STEP
    kp, vp = refs[:G], refs[G:2 * G]
    kn_ref, vn_ref, mb_ref, tab_ref, o_ref, m_ref, l_ref, acc_ref = refs[2 * G:]
    step = pl.program_id(1)

    @pl.when(step == 0)
    def _():
        m_ref[...] = jnp.full_like(m_ref, -3e38)
        l_ref[...] = jnp.zeros_like(l_ref)
        acc_ref[...] = jnp.zeros_like(acc_ref)

    def process(pages):
        for n in range(KV_HEADS):
            qn = q_ref[0, n].astype(BF16)
            s_parts = []
            for (kr, vr, cls, cb) in pages:
                mbt = mb_ref[:, cb * LANES:(cb + 1) * LANES]
                s_parts.append(_dot_nt(qn, _head_rows(kr, n).astype(BF16)) * ATT_SCALE
                               + tab_ref[cls, n] + jnp.concatenate([mbt] * ATT_GROUP, axis=0))
            s = s_parts[0] if len(s_parts) == 1 else jnp.concatenate(s_parts, axis=1)
            m_old = m_ref[n]
            m_new = jnp.maximum(m_old, jnp.max(s, axis=-1, keepdims=True))
            a = jnp.exp(m_old - m_new)
            p = jnp.exp(s - m_new)
            l_ref[n] = a * l_ref[n] + jnp.sum(p, axis=-1, keepdims=True)
            pv = None
            for j, (kr, vr, cls, cb) in enumerate(pages):
                d = _dot(p[:, j * LANES:(j + 1) * LANES].astype(BF16), _head_rows(vr, n).astype(BF16))
                pv = d if pv is None else pv + d
            acc_ref[n] = a * acc_ref[n] + pv
            m_ref[n] = m_new

    @pl.when(step < n_steps - 1)
    def _():
        process([(kp[j], vp[j], jnp.minimum(n_pages - (step * G + j), 2), j) for j in range(G)])

    @pl.when(step == n_steps - 1)
    def _():
        process([(kn_ref, vn_ref, 0, 0)])
        for n in range(KV_HEADS):
            o_ref[0, n] = acc_ref[n] / l_ref[n]


def _dsa_sample(x, cache_k, cache_v, cache_kidx, page_table, w_in, kidx_g, kidx_b, t5_bias,
                B, T, Tp, past_len):
    M = B * Tp
    G = PAGES_PER_STEP
    page = cache_kidx.shape[1]
    n_pages = past_len // page
    assert page == LANES and _far_bucket_ok(page) and n_pages % G == 0
    h_main, ik, ikw = _dsa_project(x, w_in, kidx_g, kidx_b, M)
    topk = min(TOPK_MAX, (past_len + T) // 4)
    n_steps = n_pages // G + 1
    S = n_steps * G * page
    n_pool = cache_kidx.shape[0]
    ck = cache_k.reshape(n_pool, page * KV_HEADS, ATT_DH)
    cv = cache_v.reshape(n_pool, page * KV_HEADS, ATT_DH)
    pt = page_table.reshape(-1).astype(jnp.int32)
    k_new = h_main[:, Q_W:Q_W + KV_W]
    v_new = h_main[:, Q_W + KV_W:Q_W + 2 * KV_W]

    def as_page(a, w):
        return jnp.pad(a.reshape(B, Tp, w), ((0, 0), (0, page - Tp), (0, 0))).reshape(B, -1, ATT_DH)

    iq = h_main[:, Q_W + 2 * KV_W:].reshape(B, Tp, IDX_HEADS, IDX_DH)
    iq_st = jnp.transpose(iq, (0, 2, 1, 3)).reshape(B, IDX_HEADS * Tp, IDX_DH)
    iw = ikw[:, LANES:LANES + IDX_HEADS].reshape(B, Tp, IDX_HEADS)
    iw_st = jnp.broadcast_to(jnp.transpose(iw, (0, 2, 1)).reshape(B, IDX_HEADS * Tp, 1),
                             (B, IDX_HEADS * Tp, LANES))

    def page_spec(rows, j):
        return pl.BlockSpec(
            (1, rows, ATT_DH),
            lambda b, s, p: (p[b * n_pages + jnp.minimum(s * G + j, n_pages - 1)], 0, 0))

    stacked = pl.BlockSpec((1, IDX_HEADS * Tp, LANES), lambda b, s, p: (b, 0, 0))
    scores = pl.pallas_call(
        functools.partial(_idxs_kernel, n_steps=n_steps),
        grid_spec=pltpu.PrefetchScalarGridSpec(
            num_scalar_prefetch=1,
            grid=(B, n_steps),
            in_specs=[stacked, stacked] + [page_spec(page, j) for j in range(G)]
                     + [pl.BlockSpec((1, page, IDX_DH), lambda b, s, p: (b, 0, 0))],
            out_specs=pl.BlockSpec((Tp, G * page), lambda b, s, p: (b, s)),
        ),
        out_shape=jax.ShapeDtypeStruct((M, S), F32),
        compiler_params=_cparams(("parallel", "arbitrary")),
    )(pt, iq_st, iw_st, *([cache_kidx] * G), as_page(ik, IDX_DH))
    qpos = jnp.tile(past_len + jnp.arange(Tp, dtype=jnp.int32), B)[:, None]
    mbias = _select(scores, qpos, topk, M)

    gq = ATT_GROUP * Tp
    q = h_main[:, :Q_W].reshape(B, Tp, KV_HEADS, ATT_GROUP, ATT_DH)
    q_st = jnp.transpose(q, (0, 2, 3, 1, 4)).reshape(B, KV_HEADS, gq, ATT_DH)
    qspec = pl.BlockSpec((1, KV_HEADS, gq, ATT_DH), lambda b, s, p: (b, 0, 0, 0))
    newspec = pl.BlockSpec((1, page * KV_HEADS, ATT_DH), lambda b, s, p: (b, 0, 0))
    o_st = pl.pallas_call(
        functools.partial(_attns_kernel, n_steps=n_steps, n_pages=n_pages),
        grid_spec=pltpu.PrefetchScalarGridSpec(
            num_scalar_prefetch=1,
            grid=(B, n_steps),
            in_specs=[qspec] + [page_spec(page * KV_HEADS, j) for j in range(G)] * 2
                     + [newspec, newspec,
                        pl.BlockSpec((Tp, G * page), lambda b, s, p: (b, s)),
                        pl.BlockSpec((3, KV_HEADS, gq, page), lambda b, s, p: (0, 0, 0, 0))],
            out_specs=qspec,
            scratch_shapes=[pltpu.VMEM((KV_HEADS, gq, 1), F32),
                            pltpu.VMEM((KV_HEADS, gq, 1), F32),
                            pltpu.VMEM((KV_HEADS, gq, ATT_DH), F32)],
        ),
        out_shape=jax.ShapeDtypeStruct((B, KV_HEADS, gq, ATT_DH), F32),
        compiler_params=_cparams(("parallel", "arbitrary")),
    )(pt, q_st, *([ck] * G), *([cv] * G), as_page(k_new, KV_W), as_page(v_new, KV_W), mbias,
      _bias_table(t5_bias, Tp, page))
    o = jnp.transpose(o_st.reshape(B, KV_HEADS, ATT_GROUP, Tp, ATT_DH), (0, 3, 1, 2, 4)).reshape(M, Q_W)
    return o, k_new, v_new, ik


def _router_kernel(x_ref, w_ref, b_ref, e_ref, g_ref, p_ref, cnt_ref, carry_ref):
    i = pl.program_id(0)
    tm = x_ref.shape[0]
    E = N_EXPERTS

    @pl.when(i == 0)
    def _():
        carry_ref[...] = jnp.zeros_like(carry_ref)

    logits = _dot(x_ref[...].astype(BF16), w_ref[...].astype(BF16)) + b_ref[...]
    lane = lax.broadcasted_iota(I32, (tm, E), 1).astype(F32)
    lane_o = lax.broadcasted_iota(I32, (tm, LANES), 1)
    vals, idxs = [], []
    cur = logits
    onehot = jnp.zeros((tm, E), F32)
    for _ in range(TOP_K):
        mx = jnp.max(cur, axis=-1, keepdims=True)
        ix = jnp.min(jnp.where(cur == mx, lane, float(E)), axis=-1, keepdims=True)
        hit = lane == ix
        onehot = jnp.where(hit, 1.0, onehot)
        cur = jnp.where(hit, -jnp.inf, cur)
        vals.append(mx)
        idxs.append(ix)
    ex = [jnp.exp(v - vals[0]) for v in vals]
    den = ex[0] + ex[1] + ex[2] + ex[3]
    r_i = lax.broadcasted_iota(I32, (tm, tm), 0)
    c_i = lax.broadcasted_iota(I32, (tm, tm), 1)
    lower = jnp.where(c_i < r_i, 1.0, 0.0).astype(BF16)
    prefix = _dot(lower, onehot.astype(BF16)) + carry_ref[...]
    e_out = jnp.zeros((tm, LANES), I32)
    g_out = jnp.zeros((tm, LANES), F32)
    p_out = jnp.zeros((tm, LANES), I32)
    for k in range(TOP_K):
        pos = jnp.sum(jnp.where(lane == idxs[k], prefix, 0.0), axis=-1, keepdims=True)
        e_out = jnp.where(lane_o == k, idxs[k].astype(I32), e_out)
        g_out = jnp.where(lane_o == k, ex[k] / den, g_out)
        p_out = jnp.where(lane_o == k, pos.astype(I32), p_out)
    e_ref[...] = e_out
    g_ref[...] = g_out
    p_ref[...] = p_out
    carry_ref[...] = carry_ref[...] + jnp.sum(onehot, axis=0, keepdims=True)
    cnt_ref[...] = carry_ref[...].astype(I32)


def _router(x, w_router, b_router, layer, tm):
    T, D = x.shape
    E = N_EXPERTS
    outs = pl.pallas_call(
        _router_kernel,
        grid=(T // tm,),
        in_specs=[pl.BlockSpec((tm, D), lambda i: (i, 0)),
                  pl.BlockSpec((None, D, E), lambda i: (layer, 0, 0)),
                  pl.BlockSpec((None, 1, E), lambda i: (layer, 0, 0))],
        out_specs=[pl.BlockSpec((tm, LANES), lambda i: (i, 0)),
                   pl.BlockSpec((tm, LANES), lambda i: (i, 0)),
                   pl.BlockSpec((tm, LANES), lambda i: (i, 0)),
                   pl.BlockSpec((1, E), lambda i: (0, 0))],
        out_shape=[jax.ShapeDtypeStruct((T, LANES), I32),
                   jax.ShapeDtypeStruct((T, LANES), F32),
                   jax.ShapeDtypeStruct((T, LANES), I32),
                   jax.ShapeDtypeStruct((1, E), I32)],
        scratch_shapes=[pltpu.VMEM((1, E), F32)],
        compiler_params=_cparams(("arbitrary",)),
    )(x, w_router, b_router.reshape(-1, 1, E))
    return outs


def _row_copy(src, s_row, dst, d_row, sem):
    return pltpu.make_async_copy(src.at[pl.ds(s_row, 1)], dst.at[pl.ds(d_row, 1)], sem)


def _dispatch_kernel(nb_ref, tok_ref, xa_hbm, xb_hbm, o_ref, buf_ref, sem):
    blk = pl.program_id(0)
    tm = o_ref.shape[0]
    ta = xa_hbm.shape[0]
    n_used = nb_ref[0]

    def issue(b, slot):
        def body(r, c):
            t = tok_ref[b * tm + r]

            @pl.when(t < ta)
            def _():
                _row_copy(xa_hbm, t, buf_ref.at[slot], r, sem.at[slot]).start()

            @pl.when(t >= ta)
            def _():
                _row_copy(xb_hbm, t - ta, buf_ref.at[slot], r, sem.at[slot]).start()

            return c

        lax.fori_loop(0, tm, body, 0)

    slot = blk % 2

    @pl.when(jnp.logical_and(blk == 0, n_used > 0))
    def _():
        issue(0, 0)

    @pl.when(blk + 1 < n_used)
    def _():
        issue(blk + 1, 1 - slot)

    @pl.when(blk < n_used)
    def _():
        pltpu.make_async_copy(xa_hbm.at[pl.ds(0, tm)], buf_ref.at[slot], sem.at[slot]).wait()
        o_ref[...] = buf_ref[slot].astype(o_ref.dtype)

    @pl.when(blk >= n_used)
    def _():
        o_ref[...] = jnp.zeros_like(o_ref)


def _dispatch(xa, xb, row_tok, n_used, tm):
    T, D = xa.shape
    assert T >= tm
    n_rows = row_tok.shape[0]
    return pl.pallas_call(
        _dispatch_kernel,
        grid_spec=pltpu.PrefetchScalarGridSpec(
            num_scalar_prefetch=2,
            grid=(n_rows // tm,),
            in_specs=[pl.BlockSpec(memory_space=pl.ANY), pl.BlockSpec(memory_space=pl.ANY)],
            out_specs=pl.BlockSpec((tm, D), lambda i, nb, tok: (i, 0)),
            scratch_shapes=[pltpu.VMEM((2, tm, D), xa.dtype), pltpu.SemaphoreType.DMA((2,))],
        ),
        out_shape=jax.ShapeDtypeStruct((n_rows, D), BF16),
        compiler_params=_cparams(("arbitrary",)),
    )(n_used, row_tok, xa, xb)


def _gu_kernel(be_ref, nb_ref, x_ref, wg_ref, wu_ref, bg_ref, bu_ref, o_ref, wgb_ref, wub_ref):
    rb = pl.program_id(1)
    prev = be_ref[jnp.maximum(rb - 1, 0)]
    changed = jnp.logical_or(rb == 0, be_ref[rb] != prev)

    @pl.when(jnp.logical_and(changed, rb < nb_ref[0]))
    def _():
        wgb_ref[...] = wg_ref[...].astype(BF16)
        wub_ref[...] = wu_ref[...].astype(BF16)

    @pl.when(rb < nb_ref[0])
    def _():
        xb = x_ref[...]
        gate = _dot(xb, wgb_ref[...]) + bg_ref[...]
        up = _dot(xb, wub_ref[...]) + bu_ref[...]
        gate = jnp.minimum(gate, SWIGLU_LIMIT)
        up = jnp.clip(up, -SWIGLU_LIMIT, SWIGLU_LIMIT)
        sig = 1.0 / (1.0 + jnp.exp(-(gate * SWIGLU_ALPHA)))
        o_ref[...] = ((up + 1.0) * (gate * sig)).astype(o_ref.dtype)

    @pl.when(rb >= nb_ref[0])
    def _():
        o_ref[...] = jnp.zeros_like(o_ref)


def _down_kernel(be_ref, nb_ref, h_ref, w_ref, b_ref, o_ref, wb_ref):
    rb = pl.program_id(1)
    prev = be_ref[jnp.maximum(rb - 1, 0)]
    changed = jnp.logical_or(rb == 0, be_ref[rb] != prev)

    @pl.when(jnp.logical_and(changed, rb < nb_ref[0]))
    def _():
        wb_ref[...] = w_ref[...].astype(BF16)

    @pl.when(rb < nb_ref[0])
    def _():
        o_ref[...] = _dot(h_ref[...], wb_ref[...]) + b_ref[...]

    @pl.when(rb >= nb_ref[0])
    def _():
        o_ref[...] = jnp.zeros_like(o_ref)


def _experts(xs, block_e, n_used, w_gu, b_gu, w_down, b_down, layer, tm, tn):
    n_rows, D = xs.shape
    DE = w_down.shape[2]
    NB = n_rows // tm
    ng = DE // tn
    rbc = lambda rb, nb: jnp.minimum(rb, nb[0] - 1)
    hid = pl.pallas_call(
        _gu_kernel,
        grid_spec=pltpu.PrefetchScalarGridSpec(
            num_scalar_prefetch=2,
            grid=(ng, NB),
            in_specs=[pl.BlockSpec((tm, D), lambda n, rb, be, nb: (rbc(rb, nb), 0)),
                      pl.BlockSpec((None, None, D, tn), lambda n, rb, be, nb: (layer, be[rbc(rb, nb)], 0, n)),
                      pl.BlockSpec((None, None, D, tn), lambda n, rb, be, nb: (layer, be[rbc(rb, nb)], 0, ng + n)),
                      pl.BlockSpec((None, None, 1, tn), lambda n, rb, be, nb: (layer, be[rbc(rb, nb)], 0, n)),
                      pl.BlockSpec((None, None, 1, tn), lambda n, rb, be, nb: (layer, be[rbc(rb, nb)], 0, ng + n))],
            out_specs=pl.BlockSpec((tm, tn), lambda n, rb, be, nb: (rb, n)),
            scratch_shapes=[pltpu.VMEM((D, tn), BF16), pltpu.VMEM((D, tn), BF16)],
        ),
        out_shape=jax.ShapeDtypeStruct((n_rows, DE), BF16),
        compiler_params=_cparams(("arbitrary", "arbitrary")),
    )(block_e, n_used, xs, w_gu, w_gu, b_gu.reshape(DEPTH, N_EXPERTS, 1, 2 * DE),
      b_gu.reshape(DEPTH, N_EXPERTS, 1, 2 * DE))
    nd = D // tn
    out = pl.pallas_call(
        _down_kernel,
        grid_spec=pltpu.PrefetchScalarGridSpec(
            num_scalar_prefetch=2,
            grid=(nd, NB),
            in_specs=[pl.BlockSpec((tm, DE), lambda n, rb, be, nb: (rbc(rb, nb), 0)),
                      pl.BlockSpec((None, None, DE, tn), lambda n, rb, be, nb: (layer, be[rbc(rb, nb)], 0, n)),
                      pl.BlockSpec((None, None, 1, tn), lambda n, rb, be, nb: (layer, be[rbc(rb, nb)], 0, n))],
            out_specs=pl.BlockSpec((tm, tn), lambda n, rb, be, nb: (rb, n)),
            scratch_shapes=[pltpu.VMEM((DE, tn), BF16)],
        ),
        out_shape=jax.ShapeDtypeStruct((n_rows, D), F32),
        compiler_params=_cparams(("arbitrary", "arbitrary")),
    )(block_e, n_used, hid, w_down, b_down.reshape(DEPTH, N_EXPERTS, 1, D))
    return out


def _combine_kernel(dest_ref, g_ref, x_ref, rows_hbm, lg_ref, lb_ref, o_ref, buf_ref, sem, *, tb):
    def issue(r, c):
        for k in range(TOP_K):
            _row_copy(rows_hbm, dest_ref[r * TOP_K + k], buf_ref.at[k], r, sem).start()
        return c

    lax.fori_loop(0, tb, issue, 0)
    for k in range(TOP_K):
        pltpu.make_async_copy(rows_hbm.at[pl.ds(0, tb)], buf_ref.at[k], sem).wait()
    g = g_ref[...]
    y = buf_ref[0] * g[:, 0:1]
    for k in range(1, TOP_K):
        y = y + buf_ref[k] * g[:, k:k + 1]
    o_ref[...] = _ln_rows(DEEPNORM_ALPHA * x_ref[...] + y, lg_ref[...], lb_ref[...])


def _combine(x, rows, dest_flat, gates, ln_g, ln_b, tb):
    T, D = x.shape
    return pl.pallas_call(
        functools.partial(_combine_kernel, tb=tb),
        grid=(T // tb,),
        in_specs=[pl.BlockSpec((tb * TOP_K,), lambda i: (i,), memory_space=pltpu.SMEM),
                  pl.BlockSpec((tb, LANES), lambda i: (i, 0)),
                  pl.BlockSpec((tb, D), lambda i: (i, 0)),
                  pl.BlockSpec(memory_space=pl.ANY),
                  pl.BlockSpec((1, D), lambda i: (0, 0)),
                  pl.BlockSpec((1, D), lambda i: (0, 0))],
        out_specs=pl.BlockSpec((tb, D), lambda i: (i, 0)),
        out_shape=jax.ShapeDtypeStruct((T, D), F32),
        scratch_shapes=[pltpu.VMEM((TOP_K, tb, D), F32), pltpu.SemaphoreType.DMA(())],
        compiler_params=_cparams(("arbitrary",)),
    )(dest_flat, gates, x, rows, ln_g.reshape(1, D), ln_b.reshape(1, D))


def _moe_ln(xa, xb, w_router, b_router, w_gu, b_gu, w_down, b_down, ln_g, ln_b, layer, tm, tn):
    Ta, D = xa.shape
    Tb = xb.shape[0]
    E = N_EXPERTS
    tba = min(Ta, 256)
    ea, ga, pa, ca = _router(xa, w_router, b_router, layer, tba)
    eb, gb, pb, cb = _router(xb, w_router, b_router, layer, Tb)
    ca, cb = ca[0], cb[0]
    counts = ca + cb
    padded = ((counts + tm - 1) // tm) * tm
    pend = jnp.cumsum(padded)
    pstart = pend - padded
    ea4, eb4 = ea[:, :TOP_K], eb[:, :TOP_K]
    dest_a = (pstart[ea4] + pa[:, :TOP_K]).reshape(-1).astype(jnp.int32)
    dest_b = (pstart[eb4] + ca[eb4] + pb[:, :TOP_K]).reshape(-1).astype(jnp.int32)
    nb_max = -(-((Ta + Tb) * TOP_K) // tm) + E
    blk_row0 = jnp.arange(nb_max, dtype=jnp.int32) * tm
    block_e = jnp.minimum(jnp.sum((pend[None, :] <= blk_row0[:, None]).astype(jnp.int32), axis=1),
                          E - 1).astype(jnp.int32)
    n_used = (pend[-1] // tm).astype(jnp.int32).reshape(1)
    tok = jnp.arange((Ta + Tb) * TOP_K, dtype=jnp.int32) // TOP_K
    row_tok = jnp.zeros((nb_max * tm,), jnp.int32).at[jnp.concatenate([dest_a, dest_b])].set(tok)
    xs = _dispatch(xa, xb, row_tok, n_used, tm)
    rows = _experts(xs, block_e, n_used, w_gu, b_gu, w_down, b_down, layer, tm, tn)
    return (_combine(xa, rows, dest_a, ga, ln_g, ln_b, tba),
            _combine(xb, rows, dest_b, gb, ln_g, ln_b, Tb))


def kernel(x_prompt, x_sample, state_ret, cache_k, cache_v, cache_kidx, page_table, t5_bias, ret_w_in, ret_gn_g, ret_gn_b, ret_w_o, dsa_w_in, dsa_kidx_g, dsa_kidx_b, dsa_w_o, ln_mix_g, ln_mix_b, ln_ffn_g, ln_ffn_b, moe_w_router, moe_b_router, moe_w_gu, moe_b_gu, moe_w_down, moe_b_down):
    B, T, D = x_prompt.shape
    Bs, Ts, _ = x_sample.shape
    Tsp = SUBLANES
    past_len = page_table.shape[1] * cache_k.shape[2]
    Mp = B * T
    Ms = Bs * Tsp

    def pad_s(a):
        return jnp.pad(a.reshape(Bs, Ts, -1), ((0, 0), (0, Tsp - Ts), (0, 0))).reshape(Ms, -1)

    def unpad_s(a):
        return a.reshape(Bs, Tsp, -1)[:, :Ts].reshape(Bs * Ts, -1)

    xp = x_prompt.reshape(Mp, D)
    xs = x_sample.reshape(Bs * Ts, D)
    tm_p = 1024 if Mp % 1024 == 0 else Mp
    C = RET_CHUNK if T % RET_CHUNK == 0 else T

    moe = lambda xa, xb, i: _moe_ln(
        xa, xb, moe_w_router, moe_b_router, moe_w_gu, moe_b_gu, moe_w_down, moe_b_down,
        ln_ffn_g[i], ln_ffn_b[i], i, 256, 1024)

    w_in, w_o = ret_w_in[0], ret_w_o[0]
    hp = _matmul(xp, w_in, 0, w_in.shape[1], tm_p, 512)
    op, ret_p = _retention(hp, jnp.zeros((B,) + state_ret.shape[2:], F32), 0, B, T, C, C,
                           ret_gn_g[0], ret_gn_b[0])
    mp = _matmul(op, w_o, 0, D, 512, 512)
    xp = _res_ln(xp, mp, ln_mix_g[0], ln_mix_b[0], 256)

    xs_pad = pad_s(xs)
    hs = _matmul(xs_pad, w_in, 0, w_in.shape[1], Ms, 512)
    os_, ret_s = _retention(hs, state_ret[0], past_len, Bs, Ts, Ts, Tsp, ret_gn_g[0], ret_gn_b[0])
    ms = _matmul(os_, w_o, 0, D, Ms, 512)
    xs = unpad_s(_res_ln(xs_pad, ms, ln_mix_g[0], ln_mix_b[0], Ms))

    xp, xs = moe(xp, xs, 0)

    w_in, w_o = dsa_w_in[0], dsa_w_o[0]
    ap, k_p, v_p, ik_p = _dsa_prompt(xp, w_in, dsa_kidx_g[0], dsa_kidx_b[0], t5_bias, B, T)
    mp = _matmul(ap, w_o, 0, D, tm_p, 512)
    xp = _res_ln(xp, mp, ln_mix_g[1], ln_mix_b[1], 256)

    xs_pad = pad_s(xs)
    as_, k_s, v_s, ik_s = _dsa_sample(xs_pad, cache_k[0], cache_v[0], cache_kidx[0], page_table, w_in,
                                      dsa_kidx_g[0], dsa_kidx_b[0], t5_bias, Bs, Ts, Tsp, past_len)
    ms = _matmul(as_, w_o, 0, D, Ms, 512)
    xs = unpad_s(_res_ln(xs_pad, ms, ln_mix_g[1], ln_mix_b[1], Ms))

    xp, xs = moe(xp, xs, 1)

    return (xp.reshape(B, T, D), xs.reshape(Bs, Ts, D),
            ret_p[None], ret_s[None],
            k_p.reshape(1, B, T, KV_HEADS, ATT_DH), v_p.reshape(1, B, T, KV_HEADS, ATT_DH),
            ik_p.reshape(1, B, T, IDX_DH),
            unpad_s(k_s).reshape(1, Bs, Ts, KV_HEADS, ATT_DH),
            unpad_s(v_s).reshape(1, Bs, Ts, KV_HEADS, ATT_DH),
            unpad_s(ik_s).reshape(1, Bs, Ts, IDX_DH))
```

```python
import functools
import math

import numpy as np
import jax
import jax.numpy as jnp
from jax import lax
from jax.experimental import pallas as pl
from jax.experimental.pallas import tpu as pltpu

F32 = jnp.float32
BF16 = jnp.bfloat16
I32 = jnp.int32

RET_HEADS = 8
RET_HB = 4
RET_CHUNK = 128
ROPE_BASE = 10000.0
ATT_HEADS = 16
ATT_DH = 128
KV_HEADS = 4
ATT_GROUP = ATT_HEADS // KV_HEADS
IDX_HEADS = 16
IDX_DH = 128
TOPK_MAX = 256
T5_BUCKETS = 32
T5_MAX_DIST = 128
N_EXPERTS = 32
TOP_K = 4
SWIGLU_LIMIT = 7.0
SWIGLU_ALPHA = 1.702
DEPTH = 2
DEEPNORM_ALPHA = (2.0 * DEPTH) ** 0.25
LN_EPS = 1e-5

LANES = 128
SUBLANES = 8
VMEM_LIMIT = 56 * 1024 * 1024
NEG_BIG = -1e30
INT_MIN = -(2 ** 31)


def _cparams(sem):
    return pltpu.CompilerParams(dimension_semantics=sem, vmem_limit_bytes=VMEM_LIMIT)


def _ln_rows(x, g, b):
    mu = jnp.mean(x, axis=-1, keepdims=True)
    xc = x - mu
    var = jnp.mean(xc * xc, axis=-1, keepdims=True)
    return xc * lax.rsqrt(var + LN_EPS) * g + b


def _dot(a, b):
    return jnp.dot(a, b, preferred_element_type=F32)


def _dot_nt(a, b):
    return lax.dot_general(a, b, (((1,), (1,)), ((), ())), preferred_element_type=F32)


def _mm_kernel(x_ref, w_ref, o_ref, xb_ref):
    @pl.when(pl.program_id(1) == 0)
    def _():
        xb_ref[...] = x_ref[...].astype(BF16)

    o_ref[...] = _dot(xb_ref[...], w_ref[...].astype(BF16)).astype(o_ref.dtype)


def _matmul(x, w, col0, ncols, tm, tn):
    M, K = x.shape
    assert M % tm == 0 and col0 % tn == 0 and ncols % tn == 0
    c0 = col0 // tn
    return pl.pallas_call(
        _mm_kernel,
        grid=(M // tm, ncols // tn),
        in_specs=[pl.BlockSpec((tm, K), lambda i, j: (i, 0)),
                  pl.BlockSpec((K, tn), lambda i, j: (0, j + c0))],
        out_specs=pl.BlockSpec((tm, tn), lambda i, j: (i, j)),
        out_shape=jax.ShapeDtypeStruct((M, ncols), F32),
        scratch_shapes=[pltpu.VMEM((tm, K), BF16)],
        compiler_params=_cparams(("parallel", "arbitrary")),
    )(x, w)


def _res_ln_kernel(x_ref, h_ref, g_ref, b_ref, o_ref, *, alpha):
    o_ref[...] = _ln_rows(alpha * x_ref[...] + h_ref[...], g_ref[...], b_ref[...])


def _ln_kernel(x_ref, g_ref, b_ref, o_ref):
    o_ref[...] = _ln_rows(x_ref[...], g_ref[...], b_ref[...])


def _res_ln(x, h, g, b, tm):
    M, D = x.shape
    row = pl.BlockSpec((tm, D), lambda i: (i, 0))
    par = pl.BlockSpec((1, D), lambda i: (0, 0))
    return pl.pallas_call(
        functools.partial(_res_ln_kernel, alpha=DEEPNORM_ALPHA),
        grid=(M // tm,),
        in_specs=[row, row, par, par],
        out_specs=row,
        out_shape=jax.ShapeDtypeStruct((M, D), F32),
        compiler_params=_cparams(("parallel",)),
    )(x, h, g.reshape(1, D), b.reshape(1, D))


def _ln_cols(x, col_blk, g, b, tm):
    M = x.shape[0]
    D = LANES
    par = pl.BlockSpec((1, D), lambda i: (0, 0))
    return pl.pallas_call(
        _ln_kernel,
        grid=(M // tm,),
        in_specs=[pl.BlockSpec((tm, D), lambda i: (i, col_blk)), par, par],
        out_specs=pl.BlockSpec((tm, D), lambda i: (i, 0)),
        out_shape=jax.ShapeDtypeStruct((M, D), F32),
        compiler_params=_cparams(("parallel",)),
    )(x, g.reshape(1, D), b.reshape(1, D))


def _ret_kernel(q_ref, k_ref, v_ref, g_ref, cos_ref, sin_ref, intra_ref, qd_ref, kd_ref, sd_ref,
                s0_ref, gng_ref, gnb_ref, o_ref, so_ref, st_ref, *, dk):
    c = pl.program_id(2)
    half = dk // 2
    dv = v_ref.shape[1] // RET_HB

    @pl.when(c == 0)
    def _():
        st_ref[...] = s0_ref[0]

    cos = cos_ref[...]
    sin = sin_ref[...]

    def rot(x):
        x1, x2 = x[:, :half], x[:, half:]
        return jnp.concatenate([x1 * cos - x2 * sin, x1 * sin + x2 * cos], axis=-1)

    for j in range(RET_HB):
        q = rot(q_ref[:, j * dk:(j + 1) * dk])
        k = rot(k_ref[:, j * dk:(j + 1) * dk]) * (dk ** -0.5)
        vb = v_ref[:, j * dv:(j + 1) * dv].astype(BF16)
        qb = q.astype(BF16)
        st = st_ref[j]
        scores = _dot_nt(qb, k.astype(BF16)) * intra_ref[j]
        o = _dot(scores.astype(BF16), vb)
        o = o + _dot(qb, st.astype(BF16)) * qd_ref[j]
        kd = (k * kd_ref[j]).astype(BF16)
        new_st = st * sd_ref[j] + lax.dot_general(kd, vb, (((0,), (0,)), ((), ())),
                                                  preferred_element_type=F32)
        st_ref[j] = new_st
        so_ref[0, j] = new_st
        y = _ln_rows(o, gng_ref[j], gnb_ref[j])
        g = g_ref[:, j * dv:(j + 1) * dv]
        o_ref[:, j * dv:(j + 1) * dv] = y * (g * (1.0 / (1.0 + jnp.exp(-g))))


def _ret_log_decay():
    return jnp.log1p(-jnp.exp2(-5.0 - jnp.arange(RET_HEADS, dtype=F32)))


def _retention(h, state0, pos0, B, T, C, Cp, gn_g, gn_b):
    H = RET_HEADS
    DK, DV = state0.shape[2], state0.shape[3]
    NC = T // C
    Tp = NC * Cp
    half = DK // 2
    log_g = _ret_log_decay()
    pos = (pos0 + jnp.arange(Tp, dtype=jnp.int32)).astype(F32)
    inv = ROPE_BASE ** (-jnp.arange(half, dtype=F32) / half)
    ang = pos[:, None] * inv[None, :]
    cos, sin = jnp.cos(ang), jnp.sin(ang)
    p = jnp.arange(C, dtype=F32)
    rel = p[:, None] - p[None, :]
    intra = jnp.where(rel[None] >= 0, jnp.exp(log_g[:, None, None] * jnp.maximum(rel, 0.0)[None]), 0.0)
    qd = jnp.exp((p[:, None] + 1.0) * log_g[None, :]).T[:, :, None]
    kd = jnp.exp((C - 1.0 - p)[:, None] * log_g[None, :]).T[:, :, None]
    sd = jnp.exp(C * log_g)[:, None, None]
    padc = Cp - C
    intra = jnp.pad(intra, ((0, 0), (0, padc), (0, padc)))
    qd = jnp.pad(qd, ((0, 0), (0, padc), (0, 0)))
    kd = jnp.pad(kd, ((0, 0), (0, padc), (0, 0)))

    HB = RET_HB
    assert H % HB == 0
    kb = H // HB
    vb = (2 * H * DK) // (HB * DV)
    gb = vb + H // HB
    assert vb * HB * DV == 2 * H * DK
    row = lambda b, hg, c: b * NC + c
    in_specs = [
        pl.BlockSpec((Cp, HB * DK), lambda b, hg, c: (row(b, hg, c), hg)),
        pl.BlockSpec((Cp, HB * DK), lambda b, hg, c: (row(b, hg, c), kb + hg)),
        pl.BlockSpec((Cp, HB * DV), lambda b, hg, c: (row(b, hg, c), vb + hg)),
        pl.BlockSpec((Cp, HB * DV), lambda b, hg, c: (row(b, hg, c), gb + hg)),
        pl.BlockSpec((Cp, half), lambda b, hg, c: (c, 0)),
        pl.BlockSpec((Cp, half), lambda b, hg, c: (c, 0)),
        pl.BlockSpec((HB, Cp, Cp), lambda b, hg, c: (hg, 0, 0)),
        pl.BlockSpec((HB, Cp, 1), lambda b, hg, c: (hg, 0, 0)),
        pl.BlockSpec((HB, Cp, 1), lambda b, hg, c: (hg, 0, 0)),
        pl.BlockSpec((HB, 1, 1), lambda b, hg, c: (hg, 0, 0)),
        pl.BlockSpec((1, HB, DK, DV), lambda b, hg, c: (b, hg, 0, 0)),
        pl.BlockSpec((HB, 1, DV), lambda b, hg, c: (hg, 0, 0)),
        pl.BlockSpec((HB, 1, DV), lambda b, hg, c: (hg, 0, 0)),
    ]
    out_specs = [
        pl.BlockSpec((Cp, HB * DV), lambda b, hg, c: (row(b, hg, c), hg)),
        pl.BlockSpec((1, HB, DK, DV), lambda b, hg, c: (b, hg, 0, 0)),
    ]
    o, st = pl.pallas_call(
        functools.partial(_ret_kernel, dk=DK),
        grid=(B, H // HB, NC),
        in_specs=in_specs,
        out_specs=out_specs,
        out_shape=[jax.ShapeDtypeStruct((B * Tp, H * DV), F32),
                   jax.ShapeDtypeStruct((B, H, DK, DV), F32)],
        scratch_shapes=[pltpu.VMEM((HB, DK, DV), F32)],
        compiler_params=_cparams(("parallel", "parallel", "arbitrary")),
    )(h, h, h, h, cos, sin, intra, qd, kd, sd, state0,
      gn_g.reshape(H, 1, DV), gn_b.reshape(H, 1, DV))
    return o, st


IDX_SCALE = IDX_HEADS ** -0.5 * IDX_DH ** -0.5


KEY_SB = 4 * LANES


def _select_rows(sc, qpos, topk, key_ref, o_ref, nsb=None):
    tr, S = sc.shape
    col = lax.broadcasted_iota(I32, (tr, S), 1)
    valid = col <= qpos
    bits = pltpu.bitcast(sc, I32)
    key = jnp.where(sc == 0.0, 0, bits ^ ((bits >> 31) & 0x7FFFFFFF))
    key_ref[...] = jnp.where(valid, key, INT_MIN)

    if nsb is None:
        nblk = S // LANES

        def count(pred):
            return jnp.sum(jnp.where(pred(key_ref[...]), 1.0, 0.0), axis=-1, keepdims=True)
    else:
        nblk = nsb * (KEY_SB // LANES)

        def count(pred):
            def sb_step(sb, acc):
                off = pl.multiple_of(sb * KEY_SB, KEY_SB)
                w = jnp.where(pred(key_ref[:, pl.ds(off, KEY_SB)]), 1.0, 0.0)
                for j in range(KEY_SB // LANES):
                    acc = acc + w[:, j * LANES:(j + 1) * LANES]
                return acc

            acc = lax.fori_loop(0, nsb, sb_step, jnp.zeros((tr, LANES), F32))
            return jnp.sum(acc, axis=-1, keepdims=True)

    def bit_step(i, ans):
        cand = ans | jnp.left_shift(jnp.int32(1), 31 - i)
        cand_s = cand ^ INT_MIN
        return jnp.where(count(lambda k: k >= cand_s) >= topk, cand, ans)

    ans = lax.fori_loop(0, 32, bit_step, jnp.zeros((tr, 1), I32))
    thr = ans ^ INT_MIN
    need = topk - count(lambda k: k > thr)

    r_i = lax.broadcasted_iota(I32, (LANES, LANES), 0)
    c_i = lax.broadcasted_iota(I32, (LANES, LANES), 1)
    upper = jnp.where(r_i < c_i, 1.0, 0.0).astype(BF16)

    def blk_step(j, carry):
        off = pl.multiple_of(j * LANES, LANES)
        kb = key_ref[:, pl.ds(off, LANES)]
        eq = jnp.where(kb == thr, jnp.where(kb == INT_MIN, 0.0, 1.0), 0.0)
        pre = _dot(eq.astype(BF16), upper) + carry
        take = jnp.where(pre < need, eq, 0.0)
        sel = jnp.where(kb > thr, 1.0, take)
        o_ref[:, pl.ds(off, LANES)] = jnp.where(sel > 0.5, 0.0, NEG_BIG)
        return carry + jnp.sum(eq, axis=-1, keepdims=True)

    lax.fori_loop(0, nblk, blk_step, jnp.zeros((tr, 1), F32))


def _select_kernel(sc_ref, qpos_ref, o_ref, key_ref, *, topk):
    _select_rows(sc_ref[...], qpos_ref[...], topk, key_ref, o_ref)


def _select(scores, qpos, topk, tr):
    R, S = scores.shape
    return pl.pallas_call(
        functools.partial(_select_kernel, topk=topk),
        grid=(R // tr,),
        in_specs=[pl.BlockSpec((tr, S), lambda i: (i, 0)),
                  pl.BlockSpec((tr, 1), lambda i: (i, 0))],
        out_specs=pl.BlockSpec((tr, S), lambda i: (i, 0)),
        out_shape=jax.ShapeDtypeStruct((R, S), F32),
        scratch_shapes=[pltpu.VMEM((tr, S), I32)],
        compiler_params=_cparams(("parallel",)),
    )(scores, qpos)


def _t5_bucket(dist):
    max_exact = T5_BUCKETS // 2
    is_small = dist < max_exact
    d = jnp.maximum(dist, 1).astype(F32)
    large = max_exact + (jnp.log(d / max_exact) / math.log(T5_MAX_DIST / max_exact)
                         * (T5_BUCKETS - max_exact)).astype(jnp.int32)
    large = jnp.minimum(large, T5_BUCKETS - 1)
    return jnp.where(is_small, dist, large)


def _bias_table(t5_bias, tq, tk):
    i = jnp.arange(tq, dtype=jnp.int32)[:, None]
    j = jnp.arange(tk, dtype=jnp.int32)[None, :]
    tabs = []
    for c in range(3):
        dist = jnp.maximum(c * tk + i - j, 0)
        tabs.append(jnp.transpose(t5_bias[_t5_bucket(dist)], (2, 0, 1)))
    return jnp.stack(tabs).astype(F32).reshape(3, KV_HEADS, ATT_GROUP * tq, tk)


def _far_bucket_ok(tk):
    d = np.arange(tk + 1, 8 * tk + 2).astype(np.float32)
    me = T5_BUCKETS // 2
    large = me + (np.log(d / me) / math.log(T5_MAX_DIST / me) * (T5_BUCKETS - me)).astype(np.int32)
    return bool(np.all(np.minimum(large, T5_BUCKETS - 1) == T5_BUCKETS - 1))


Q_W = ATT_HEADS * ATT_DH
KV_W = KV_HEADS * ATT_DH
IQ_W = IDX_HEADS * IDX_DH
DSA_MAIN = Q_W + 2 * KV_W + IQ_W
ATT_SCALE = ATT_DH ** -0.5


def _dsa_project(x, w_in, kidx_g, kidx_b, tm):
    h_main = _matmul(x, w_in, 0, DSA_MAIN, tm, 512)
    tail = w_in.shape[1] - DSA_MAIN
    w_tail = jnp.pad(w_in[:, DSA_MAIN:], ((0, 0), (0, 2 * LANES - tail)))
    ikw = _matmul(x, w_tail, 0, 2 * LANES, tm, 2 * LANES)
    ik = _ln_cols(ikw, 0, kidx_g, kidx_b, min(tm, 512))
    return h_main, ik, ikw


def _dsap_kernel(iq0_ref, iq1_ref, iw_ref, ik_ref, q_ref, k_ref, v_ref, tab_ref, o_ref,
                 iqs_ref, iwb_ref, sc_ref, key_ref, mb_ref, qs_ref, s_ref, m_ref, l_ref, acc_ref,
                 *, topk):
    qi = pl.program_id(1)
    tq = q_ref.shape[0]
    half = IDX_HEADS // 2
    nj = KEY_SB // LANES
    nsb = (qi + nj) // nj

    iw = iw_ref[...]
    for hh in range(IDX_HEADS):
        src = iq0_ref if hh < half else iq1_ref
        j = hh % half
        iqs_ref[hh * tq:(hh + 1) * tq, :] = src[:, j * IDX_DH:(j + 1) * IDX_DH].astype(BF16)
        iwb_ref[hh * tq:(hh + 1) * tq, :] = jnp.broadcast_to(iw[:, hh:hh + 1] * IDX_SCALE, (tq, LANES))

    sc_ref[...] = jnp.zeros_like(sc_ref)

    def sc_sb(sb, c):
        for j in range(nj):
            off = pl.multiple_of(sb * KEY_SB + j * LANES, LANES)
            ikb = ik_ref[pl.ds(off, LANES), :].astype(BF16)
            r = jnp.maximum(_dot_nt(iqs_ref[...], ikb), 0.0) * iwb_ref[...]
            acc = r[0:tq]
            for hh in range(1, IDX_HEADS):
                acc = acc + r[hh * tq:(hh + 1) * tq]
            sc_ref[:, pl.ds(off, LANES)] = acc
        return c

    lax.fori_loop(0, nsb, sc_sb, 0)

    qpos = qi * tq + lax.broadcasted_iota(I32, (tq, 1), 0)
    _select_rows(sc_ref[...], qpos, topk, key_ref, mb_ref, nsb)

    for n in range(KV_HEADS):
        for g in range(ATT_GROUP):
            hh = n * ATT_GROUP + g
            qs_ref[g * tq:(g + 1) * tq, :] = q_ref[:, hh * ATT_DH:(hh + 1) * ATT_DH].astype(BF16)
        m_ref[...] = jnp.full_like(m_ref, -3e38)

        def stage_a(sb, c):
            off = pl.multiple_of(sb * KEY_SB, KEY_SB)
            kblk = k_ref[pl.ds(off, KEY_SB), n * ATT_DH:(n + 1) * ATT_DH].astype(BF16)
            s = _dot_nt(qs_ref[...], kblk) * ATT_SCALE
            m = m_ref[...]
            for j in range(nj):
                offj = pl.multiple_of(sb * KEY_SB + j * LANES, LANES)
                mbt = mb_ref[:, pl.ds(offj, LANES)]
                cls = jnp.clip(qi - (sb * nj + j), 0, 2)
                sj = (s[:, j * LANES:(j + 1) * LANES] + tab_ref[cls, n]
                      + jnp.concatenate([mbt] * ATT_GROUP, axis=0))
                s_ref[:, pl.ds(offj, LANES)] = sj
                m = jnp.maximum(m, sj)
            m_ref[...] = m
            return c

        lax.fori_loop(0, nsb, stage_a, 0)
        m_ref[...] = jnp.broadcast_to(jnp.max(m_ref[...], axis=-1, keepdims=True), m_ref.shape)
        l_ref[...] = jnp.zeros_like(l_ref)
        acc_ref[...] = jnp.zeros_like(acc_ref)

        def stage_b(sb, c):
            off = pl.multiple_of(sb * KEY_SB, KEY_SB)
            m = m_ref[...]
            l = l_ref[...]
            ps = []
            for j in range(nj):
                offj = pl.multiple_of(sb * KEY_SB + j * LANES, LANES)
                pj = jnp.exp(s_ref[:, pl.ds(offj, LANES)] - m)
                l = l + pj
                ps.append(pj.astype(BF16))
            l_ref[...] = l
            vblk = v_ref[pl.ds(off, KEY_SB), n * ATT_DH:(n + 1) * ATT_DH].astype(BF16)
            acc_ref[...] += _dot(jnp.concatenate(ps, axis=1), vblk)
            return c

        lax.fori_loop(0, nsb, stage_b, 0)
        o = acc_ref[...] / jnp.sum(l_ref[...], axis=-1, keepdims=True)
        for g in range(ATT_GROUP):
            hh = n * ATT_GROUP + g
            o_ref[:, hh * ATT_DH:(hh + 1) * ATT_DH] = o[g * tq:(g + 1) * tq]


def _dsa_prompt(x, w_in, kidx_g, kidx_b, t5_bias, B, T):
    M = B * T
    h_main, ik, ikw = _dsa_project(x, w_in, kidx_g, kidx_b, 1024 if M % 1024 == 0 else M)
    topk = min(TOPK_MAX, T // 4)
    tq = LANES
    assert _far_bucket_ok(LANES) and T % KEY_SB == 0
    nq = T // tq
    iqc = (Q_W + 2 * KV_W) // (IQ_W // 2)
    assert iqc * (IQ_W // 2) == Q_W + 2 * KV_W
    kcol = Q_W // KV_W
    gq = ATT_GROUP * tq
    o = pl.pallas_call(
        functools.partial(_dsap_kernel, topk=topk),
        grid=(B, nq),
        in_specs=[pl.BlockSpec((tq, IQ_W // 2), lambda b, qi: (b * nq + qi, iqc)),
                  pl.BlockSpec((tq, IQ_W // 2), lambda b, qi: (b * nq + qi, iqc + 1)),
                  pl.BlockSpec((tq, LANES), lambda b, qi: (b * nq + qi, 1)),
                  pl.BlockSpec((T, IDX_DH), lambda b, qi: (b, 0)),
                  pl.BlockSpec((tq, Q_W), lambda b, qi: (b * nq + qi, 0)),
                  pl.BlockSpec((T, KV_W), lambda b, qi: (b, kcol)),
                  pl.BlockSpec((T, KV_W), lambda b, qi: (b, kcol + 1)),
                  pl.BlockSpec((3, KV_HEADS, gq, LANES), lambda b, qi: (0, 0, 0, 0))],
        out_specs=pl.BlockSpec((tq, Q_W), lambda b, qi: (b * nq + qi, 0)),
        out_shape=jax.ShapeDtypeStruct((M, Q_W), F32),
        scratch_shapes=[pltpu.VMEM((IDX_HEADS * tq, IDX_DH), BF16),
                        pltpu.VMEM((IDX_HEADS * tq, LANES), F32),
                        pltpu.VMEM((tq, T), F32),
                        pltpu.VMEM((tq, T), I32),
                        pltpu.VMEM((tq, T), F32),
                        pltpu.VMEM((gq, ATT_DH), BF16),
                        pltpu.VMEM((gq, T), F32),
                        pltpu.VMEM((gq, LANES), F32),
                        pltpu.VMEM((gq, LANES), F32),
                        pltpu.VMEM((gq, ATT_DH), F32)],
        compiler_params=_cparams(("parallel", "arbitrary")),
    )(h_main, h_main, ikw, ik, h_main, h_main, h_main, _bias_table(t5_bias, tq, LANES))
    k_new = h_main[:, Q_W:Q_W + KV_W]
    v_new = h_main[:, Q_W + KV_W:Q_W + 2 * KV_W]
    return o, k_new, v_new, ik


PAGES_PER_STEP = 16


def _head_rows(ref, n):
    return ref[0, pl.ds(n, LANES, stride=KV_HEADS), :]


def _idxs_kernel(pt_ref, iq_ref, iw_ref, *refs, n_steps):
    G = PAGES_PER_STEP
    page_refs, kn_ref, o_ref = refs[:G], refs[G], refs[G + 1]
    step = pl.program_id(1)
    tp = o_ref.shape[0]
    iq = iq_ref[0].astype(BF16)
    iwb = iw_ref[0] * IDX_SCALE

    def scores(keys):
        r = jnp.maximum(_dot_nt(iq, keys.astype(BF16)), 0.0) * iwb
        acc = r[0:tp]
        for hh in range(1, IDX_HEADS):
            acc = acc + r[hh * tp:(hh + 1) * tp]
        return acc

    @pl.when(step < n_steps - 1)
    def _():
        for j in range(G):
            o_ref[:, j * LANES:(j + 1) * LANES] = scores(page_refs[j][0])

    @pl.when(step == n_steps - 1)
    def _():
        o_ref[...] = jnp.zeros_like(o_ref)
        o_ref[:, 0:LANES] = scores(kn_ref[0])


def _attns_kernel(pt_ref, q_ref, *refs, n_steps, n_pages):
    G = PAGES_PER_STEP
    kp, vp = refs[:G], refs[G:2 * G]
    kn_ref, vn_ref, mb_ref, tab_ref, o_ref, m_ref, l_ref, acc_ref = refs[2 * G:]
    step = pl.program_id(1)

    @pl.when(step == 0)
    def _():
        m_ref[...] = jnp.full_like(m_ref, -3e38)
        l_ref[...] = jnp.zeros_like(l_ref)
        acc_ref[...] = jnp.zeros_like(acc_ref)

    def process(pages):
        for n in range(KV_HEADS):
            qn = q_ref[0, n].astype(BF16)
            s_parts = []
            for (kr, vr, cls, cb) in pages:
                mbt = mb_ref[:, cb * LANES:(cb + 1) * LANES]
                s_parts.append(_dot_nt(qn, _head_rows(kr, n).astype(BF16)) * ATT_SCALE
                               + tab_ref[cls, n] + jnp.concatenate([mbt] * ATT_GROUP, axis=0))
            s = s_parts[0] if len(s_parts) == 1 else jnp.concatenate(s_parts, axis=1)
            m_old = m_ref[n]
            m_new = jnp.maximum(m_old, jnp.max(s, axis=-1, keepdims=True))
            a = jnp.exp(m_old - m_new)
            p = jnp.exp(s - m_new)
            l_ref[n] = a * l_ref[n] + jnp.sum(p, axis=-1, keepdims=True)
            pv = None
            for j, (kr, vr, cls, cb) in enumerate(pages):
                d = _dot(p[:, j * LANES:(j + 1) * LANES].astype(BF16), _head_rows(vr, n).astype(BF16))
                pv = d if pv is None else pv + d
            acc_ref[n] = a * acc_ref[n] + pv
            m_ref[n] = m_new

    @pl.when(step < n_steps - 1)
    def _():
        process([(kp[j], vp[j], jnp.minimum(n_pages - (step * G + j), 2), j) for j in range(G)])

    @pl.when(step == n_steps - 1)
    def _():
        process([(kn_ref, vn_ref, 0, 0)])
        for n in range(KV_HEADS):
            o_ref[0, n] = acc_ref[n] / l_ref[n]


def _dsa_sample(x, cache_k, cache_v, cache_kidx, page_table, w_in, kidx_g, kidx_b, t5_bias,
                B, T, Tp, past_len):
    M = B * Tp
    G = PAGES_PER_STEP
    page = cache_kidx.shape[1]
    n_pages = past_len // page
    assert page == LANES and _far_bucket_ok(page) and n_pages % G == 0
    h_main, ik, ikw = _dsa_project(x, w_in, kidx_g, kidx_b, M)
    topk = min(TOPK_MAX, (past_len + T) // 4)
    n_steps = n_pages // G + 1
    S = n_steps * G * page
    n_pool = cache_kidx.shape[0]
    ck = cache_k.reshape(n_pool, page * KV_HEADS, ATT_DH)
    cv = cache_v.reshape(n_pool, page * KV_HEADS, ATT_DH)
    pt = page_table.reshape(-1).astype(jnp.int32)
    k_new = h_main[:, Q_W:Q_W + KV_W]
    v_new = h_main[:, Q_W + KV_W:Q_W + 2 * KV_W]

    def as_page(a, w):
        return jnp.pad(a.reshape(B, Tp, w), ((0, 0), (0, page - Tp), (0, 0))).reshape(B, -1, ATT_DH)

    iq = h_main[:, Q_W + 2 * KV_W:].reshape(B, Tp, IDX_HEADS, IDX_DH)
    iq_st = jnp.transpose(iq, (0, 2, 1, 3)).reshape(B, IDX_HEADS * Tp, IDX_DH)
    iw = ikw[:, LANES:LANES + IDX_HEADS].reshape(B, Tp, IDX_HEADS)
    iw_st = jnp.broadcast_to(jnp.transpose(iw, (0, 2, 1)).reshape(B, IDX_HEADS * Tp, 1),
                             (B, IDX_HEADS * Tp, LANES))

    def page_spec(rows, j):
        return pl.BlockSpec(
            (1, rows, ATT_DH),
            lambda b, s, p: (p[b * n_pages + jnp.minimum(s * G + j, n_pages - 1)], 0, 0))

    stacked = pl.BlockSpec((1, IDX_HEADS * Tp, LANES), lambda b, s, p: (b, 0, 0))
    scores = pl.pallas_call(
        functools.partial(_idxs_kernel, n_steps=n_steps),
        grid_spec=pltpu.PrefetchScalarGridSpec(
            num_scalar_prefetch=1,
            grid=(B, n_steps),
            in_specs=[stacked, stacked] + [page_spec(page, j) for j in range(G)]
                     + [pl.BlockSpec((1, page, IDX_DH), lambda b, s, p: (b, 0, 0))],
            out_specs=pl.BlockSpec((Tp, G * page), lambda b, s, p: (b, s)),
        ),
        out_shape=jax.ShapeDtypeStruct((M, S), F32),
        compiler_params=_cparams(("parallel", "arbitrary")),
    )(pt, iq_st, iw_st, *([cache_kidx] * G), as_page(ik, IDX_DH))
    qpos = jnp.tile(past_len + jnp.arange(Tp, dtype=jnp.int32), B)[:, None]
    mbias = _select(scores, qpos, topk, M)

    gq = ATT_GROUP * Tp
    q = h_main[:, :Q_W].reshape(B, Tp, KV_HEADS, ATT_GROUP, ATT_DH)
    q_st = jnp.transpose(q, (0, 2, 3, 1, 4)).reshape(B, KV_HEADS, gq, ATT_DH)
    qspec = pl.BlockSpec((1, KV_HEADS, gq, ATT_DH), lambda b, s, p: (b, 0, 0, 0))
    newspec = pl.BlockSpec((1, page * KV_HEADS, ATT_DH), lambda b, s, p: (b, 0, 0))
    o_st = pl.pallas_call(
        functools.partial(_attns_kernel, n_steps=n_steps, n_pages=n_pages),
        grid_spec=pltpu.PrefetchScalarGridSpec(
            num_scalar_prefetch=1,
            grid=(B, n_steps),
            in_specs=[qspec] + [page_spec(page * KV_HEADS, j) for j in range(G)] * 2
                     + [newspec, newspec,
                        pl.BlockSpec((Tp, G * page), lambda b, s, p: (b, s)),
                        pl.BlockSpec((3, KV_HEADS, gq, page), lambda b, s, p: (0, 0, 0, 0))],
            out_specs=qspec,
            scratch_shapes=[pltpu.VMEM((KV_HEADS, gq, 1), F32),
                            pltpu.VMEM((KV_HEADS, gq, 1), F32),
                            pltpu.VMEM((KV_HEADS, gq, ATT_DH), F32)],
        ),
        out_shape=jax.ShapeDtypeStruct((B, KV_HEADS, gq, ATT_DH), F32),
        compiler_params=_cparams(("parallel", "arbitrary")),
    )(pt, q_st, *([ck] * G), *([cv] * G), as_page(k_new, KV_W), as_page(v_new, KV_W), mbias,
      _bias_table(t5_bias, Tp, page))
    o = jnp.transpose(o_st.reshape(B, KV_HEADS, ATT_GROUP, Tp, ATT_DH), (0, 3, 1, 2, 4)).reshape(M, Q_W)
    return o, k_new, v_new, ik


def _router_kernel(x_ref, w_ref, b_ref, e_ref, g_ref, p_ref, cnt_ref, carry_ref):
    i = pl.program_id(0)
    tm = x_ref.shape[0]
    E = N_EXPERTS

    @pl.when(i == 0)
    def _():
        carry_ref[...] = jnp.zeros_like(carry_ref)

    logits = _dot(x_ref[...].astype(BF16), w_ref[...].astype(BF16)) + b_ref[...]
    lane = lax.broadcasted_iota(I32, (tm, E), 1).astype(F32)
    lane_o = lax.broadcasted_iota(I32, (tm, LANES), 1)
    vals, idxs = [], []
    cur = logits
    onehot = jnp.zeros((tm, E), F32)
    for _ in range(TOP_K):
        mx = jnp.max(cur, axis=-1, keepdims=True)
        ix = jnp.min(jnp.where(cur == mx, lane, float(E)), axis=-1, keepdims=True)
        hit = lane == ix
        onehot = jnp.where(hit, 1.0, onehot)
        cur = jnp.where(hit, -jnp.inf, cur)
        vals.append(mx)
        idxs.append(ix)
    ex = [jnp.exp(v - vals[0]) for v in vals]
    den = ex[0] + ex[1] + ex[2] + ex[3]
    r_i = lax.broadcasted_iota(I32, (tm, tm), 0)
    c_i = lax.broadcasted_iota(I32, (tm, tm), 1)
    lower = jnp.where(c_i < r_i, 1.0, 0.0).astype(BF16)
    prefix = _dot(lower, onehot.astype(BF16)) + carry_ref[...]
    e_out = jnp.zeros((tm, LANES), I32)
    g_out = jnp.zeros((tm, LANES), F32)
    p_out = jnp.zeros((tm, LANES), I32)
    for k in range(TOP_K):
        pos = jnp.sum(jnp.where(lane == idxs[k], prefix, 0.0), axis=-1, keepdims=True)
        e_out = jnp.where(lane_o == k, idxs[k].astype(I32), e_out)
        g_out = jnp.where(lane_o == k, ex[k] / den, g_out)
        p_out = jnp.where(lane_o == k, pos.astype(I32), p_out)
    e_ref[...] = e_out
    g_ref[...] = g_out
    p_ref[...] = p_out
    carry_ref[...] = carry_ref[...] + jnp.sum(onehot, axis=0, keepdims=True)
    cnt_ref[...] = carry_ref[...].astype(I32)


def _router(x, w_router, b_router, layer, tm):
    T, D = x.shape
    E = N_EXPERTS
    outs = pl.pallas_call(
        _router_kernel,
        grid=(T // tm,),
        in_specs=[pl.BlockSpec((tm, D), lambda i: (i, 0)),
                  pl.BlockSpec((None, D, E), lambda i: (layer, 0, 0)),
                  pl.BlockSpec((None, 1, E), lambda i: (layer, 0, 0))],
        out_specs=[pl.BlockSpec((tm, LANES), lambda i: (i, 0)),
                   pl.BlockSpec((tm, LANES), lambda i: (i, 0)),
                   pl.BlockSpec((tm, LANES), lambda i: (i, 0)),
                   pl.BlockSpec((1, E), lambda i: (0, 0))],
        out_shape=[jax.ShapeDtypeStruct((T, LANES), I32),
                   jax.ShapeDtypeStruct((T, LANES), F32),
                   jax.ShapeDtypeStruct((T, LANES), I32),
                   jax.ShapeDtypeStruct((1, E), I32)],
        scratch_shapes=[pltpu.VMEM((1, E), F32)],
        compiler_params=_cparams(("arbitrary",)),
    )(x, w_router, b_router.reshape(-1, 1, E))
    return outs


ISSUE_UNROLL = 8


def _row_copy(src, s_row, dst, d_row, sem):
    return pltpu.make_async_copy(src.at[pl.ds(s_row, 1)], dst.at[pl.ds(d_row, 1)], sem)


def _dispatch_kernel(nb_ref, tok_ref, x_hbm, o_ref, buf_ref, sem):
    blk = pl.program_id(0)
    tm = o_ref.shape[0]
    n_used = nb_ref[0]

    def issue(b, slot):
        def body(r, c):
            _row_copy(x_hbm, tok_ref[b * tm + r], buf_ref.at[slot], r, sem.at[slot]).start()
            return c

        lax.fori_loop(0, tm, body, 0, unroll=ISSUE_UNROLL)

    slot = blk % 2

    @pl.when(jnp.logical_and(blk == 0, n_used > 0))
    def _():
        issue(0, 0)

    @pl.when(blk + 1 < n_used)
    def _():
        issue(blk + 1, 1 - slot)

    @pl.when(blk < n_used)
    def _():
        pltpu.make_async_copy(x_hbm.at[pl.ds(0, tm)], buf_ref.at[slot], sem.at[slot]).wait()
        o_ref[...] = buf_ref[slot].astype(o_ref.dtype)

    @pl.when(blk >= n_used)
    def _():
        o_ref[...] = jnp.zeros_like(o_ref)


def _dispatch(x, row_tok, n_used, tm):
    T, D = x.shape
    assert T >= tm
    n_rows = row_tok.shape[0]
    return pl.pallas_call(
        _dispatch_kernel,
        grid_spec=pltpu.PrefetchScalarGridSpec(
            num_scalar_prefetch=2,
            grid=(n_rows // tm,),
            in_specs=[pl.BlockSpec(memory_space=pl.ANY)],
            out_specs=pl.BlockSpec((tm, D), lambda i, nb, tok: (i, 0)),
            scratch_shapes=[pltpu.VMEM((2, tm, D), x.dtype), pltpu.SemaphoreType.DMA((2,))],
        ),
        out_shape=jax.ShapeDtypeStruct((n_rows, D), BF16),
        compiler_params=_cparams(("arbitrary",)),
    )(n_used, row_tok, x)


def _gu_kernel(be_ref, nb_ref, x_ref, wg_ref, wu_ref, bg_ref, bu_ref, o_ref, wgb_ref, wub_ref):
    rb = pl.program_id(1)
    prev = be_ref[jnp.maximum(rb - 1, 0)]
    changed = jnp.logical_or(rb == 0, be_ref[rb] != prev)

    @pl.when(jnp.logical_and(changed, rb < nb_ref[0]))
    def _():
        wgb_ref[...] = wg_ref[...].astype(BF16)
        wub_ref[...] = wu_ref[...].astype(BF16)

    @pl.when(rb < nb_ref[0])
    def _():
        xb = x_ref[...]
        gate = _dot(xb, wgb_ref[...]) + bg_ref[...]
        up = _dot(xb, wub_ref[...]) + bu_ref[...]
        gate = jnp.minimum(gate, SWIGLU_LIMIT)
        up = jnp.clip(up, -SWIGLU_LIMIT, SWIGLU_LIMIT)
        sig = 1.0 / (1.0 + jnp.exp(-(gate * SWIGLU_ALPHA)))
        o_ref[...] = ((up + 1.0) * (gate * sig)).astype(o_ref.dtype)

    @pl.when(rb >= nb_ref[0])
    def _():
        o_ref[...] = jnp.zeros_like(o_ref)


def _down_kernel(be_ref, nb_ref, h_ref, w_ref, b_ref, o_ref, wb_ref):
    rb = pl.program_id(1)
    prev = be_ref[jnp.maximum(rb - 1, 0)]
    changed = jnp.logical_or(rb == 0, be_ref[rb] != prev)

    @pl.when(jnp.logical_and(changed, rb < nb_ref[0]))
    def _():
        wb_ref[...] = w_ref[...].astype(BF16)

    @pl.when(rb < nb_ref[0])
    def _():
        o_ref[...] = _dot(h_ref[...], wb_ref[...]) + b_ref[...]

    @pl.when(rb >= nb_ref[0])
    def _():
        o_ref[...] = jnp.zeros_like(o_ref)


def _experts(xs, block_e, n_used, w_gu, b_gu, w_down, b_down, layer, tm, tn):
    n_rows, D = xs.shape
    DE = w_down.shape[2]
    NB = n_rows // tm
    ng = DE // tn
    rbc = lambda rb, nb: jnp.minimum(rb, nb[0] - 1)
    hid = pl.pallas_call(
        _gu_kernel,
        grid_spec=pltpu.PrefetchScalarGridSpec(
            num_scalar_prefetch=2,
            grid=(ng, NB),
            in_specs=[pl.BlockSpec((tm, D), lambda n, rb, be, nb: (rbc(rb, nb), 0)),
                      pl.BlockSpec((None, None, D, tn), lambda n, rb, be, nb: (layer, be[rbc(rb, nb)], 0, n)),
                      pl.BlockSpec((None, None, D, tn), lambda n, rb, be, nb: (layer, be[rbc(rb, nb)], 0, ng + n)),
                      pl.BlockSpec((None, None, 1, tn), lambda n, rb, be, nb: (layer, be[rbc(rb, nb)], 0, n)),
                      pl.BlockSpec((None, None, 1, tn), lambda n, rb, be, nb: (layer, be[rbc(rb, nb)], 0, ng + n))],
            out_specs=pl.BlockSpec((tm, tn), lambda n, rb, be, nb: (rb, n)),
            scratch_shapes=[pltpu.VMEM((D, tn), BF16), pltpu.VMEM((D, tn), BF16)],
        ),
        out_shape=jax.ShapeDtypeStruct((n_rows, DE), BF16),
        compiler_params=_cparams(("arbitrary", "arbitrary")),
    )(block_e, n_used, xs, w_gu, w_gu, b_gu.reshape(DEPTH, N_EXPERTS, 1, 2 * DE),
      b_gu.reshape(DEPTH, N_EXPERTS, 1, 2 * DE))
    tn = D
    nd = D // tn
    out = pl.pallas_call(
        _down_kernel,
        grid_spec=pltpu.PrefetchScalarGridSpec(
            num_scalar_prefetch=2,
            grid=(nd, NB),
            in_specs=[pl.BlockSpec((tm, DE), lambda n, rb, be, nb: (rbc(rb, nb), 0)),
                      pl.BlockSpec((None, None, DE, tn), lambda n, rb, be, nb: (layer, be[rbc(rb, nb)], 0, n)),
                      pl.BlockSpec((None, None, 1, tn), lambda n, rb, be, nb: (layer, be[rbc(rb, nb)], 0, n))],
            out_specs=pl.BlockSpec((tm, tn), lambda n, rb, be, nb: (rb, n)),
            scratch_shapes=[pltpu.VMEM((DE, tn), BF16)],
        ),
        out_shape=jax.ShapeDtypeStruct((n_rows, D), F32),
        compiler_params=_cparams(("arbitrary", "arbitrary")),
    )(block_e, n_used, hid, w_down, b_down.reshape(DEPTH, N_EXPERTS, 1, D))
    return out


def _combine_kernel(dest_ref, g_ref, x_ref, rows_hbm, lg_ref, lb_ref, o_ref, buf_ref, sem, *, tb):
    def issue(r, c):
        for k in range(TOP_K):
            _row_copy(rows_hbm, dest_ref[r * TOP_K + k], buf_ref.at[k], r, sem).start()
        return c

    lax.fori_loop(0, tb, issue, 0, unroll=ISSUE_UNROLL // TOP_K)
    for k in range(TOP_K):
        pltpu.make_async_copy(rows_hbm.at[pl.ds(0, tb)], buf_ref.at[k], sem).wait()
    g = g_ref[...]
    y = buf_ref[0] * g[:, 0:1]
    for k in range(1, TOP_K):
        y = y + buf_ref[k] * g[:, k:k + 1]
    o_ref[...] = _ln_rows(DEEPNORM_ALPHA * x_ref[...] + y, lg_ref[...], lb_ref[...])


def _combine(x, rows, dest_flat, gates, ln_g, ln_b, tb):
    T, D = x.shape
    return pl.pallas_call(
        functools.partial(_combine_kernel, tb=tb),
        grid=(T // tb,),
        in_specs=[pl.BlockSpec((tb * TOP_K,), lambda i: (i,), memory_space=pltpu.SMEM),
                  pl.BlockSpec((tb, LANES), lambda i: (i, 0)),
                  pl.BlockSpec((tb, D), lambda i: (i, 0)),
                  pl.BlockSpec(memory_space=pl.ANY),
                  pl.BlockSpec((1, D), lambda i: (0, 0)),
                  pl.BlockSpec((1, D), lambda i: (0, 0))],
        out_specs=pl.BlockSpec((tb, D), lambda i: (i, 0)),
        out_shape=jax.ShapeDtypeStruct((T, D), F32),
        scratch_shapes=[pltpu.VMEM((TOP_K, tb, D), F32), pltpu.SemaphoreType.DMA(())],
        compiler_params=_cparams(("arbitrary",)),
    )(dest_flat, gates, x, rows, ln_g.reshape(1, D), ln_b.reshape(1, D))


def _moe_ln(xa, xb, w_router, b_router, w_gu, b_gu, w_down, b_down, ln_g, ln_b, layer, tm, tn):
    Ta, D = xa.shape
    Tb = xb.shape[0]
    E = N_EXPERTS
    tba = min(Ta, 256)
    ea, ga, pa, ca = _router(xa, w_router, b_router, layer, tba)
    eb, gb, pb, cb = _router(xb, w_router, b_router, layer, Tb)
    ca, cb = ca[0], cb[0]
    counts = ca + cb
    padded = ((counts + tm - 1) // tm) * tm
    pend = jnp.cumsum(padded)
    pstart = pend - padded
    ea4, eb4 = ea[:, :TOP_K], eb[:, :TOP_K]

    def lookup(table, idx):
        hit = idx[..., None] == jnp.arange(E, dtype=jnp.int32)
        return jnp.sum(jnp.where(hit, table, 0), axis=-1)

    dest_a = (lookup(pstart, ea4) + pa[:, :TOP_K]).reshape(-1).astype(jnp.int32)
    dest_b = (lookup(pstart + ca, eb4) + pb[:, :TOP_K]).reshape(-1).astype(jnp.int32)
    nb_max = -(-((Ta + Tb) * TOP_K) // tm) + E
    blk_row0 = jnp.arange(nb_max, dtype=jnp.int32) * tm
    block_e = jnp.minimum(jnp.sum((pend[None, :] <= blk_row0[:, None]).astype(jnp.int32), axis=1),
                          E - 1).astype(jnp.int32)
    n_used = (pend[-1] // tm).astype(jnp.int32).reshape(1)
    tok = jnp.arange((Ta + Tb) * TOP_K, dtype=jnp.int32) // TOP_K
    row_tok = jnp.zeros((nb_max * tm,), jnp.int32).at[jnp.concatenate([dest_a, dest_b])].set(tok)
    xs = _dispatch(jnp.concatenate([xa, xb], axis=0), row_tok, n_used, tm)
    rows = _experts(xs, block_e, n_used, w_gu, b_gu, w_down, b_down, layer, tm, tn)
    return (_combine(xa, rows, dest_a, ga, ln_g, ln_b, tba),
            _combine(xb, rows, dest_b, gb, ln_g, ln_b, Tb))


def kernel(x_prompt, x_sample, state_ret, cache_k, cache_v, cache_kidx, page_table, t5_bias, ret_w_in, ret_gn_g, ret_gn_b, ret_w_o, dsa_w_in, dsa_kidx_g, dsa_kidx_b, dsa_w_o, ln_mix_g, ln_mix_b, ln_ffn_g, ln_ffn_b, moe_w_router, moe_b_router, moe_w_gu, moe_b_gu, moe_w_down, moe_b_down):
    B, T, D = x_prompt.shape
    Bs, Ts, _ = x_sample.shape
    Tsp = SUBLANES
    past_len = page_table.shape[1] * cache_k.shape[2]
    Mp = B * T
    Ms = Bs * Tsp

    def pad_s(a):
        return jnp.pad(a.reshape(Bs, Ts, -1), ((0, 0), (0, Tsp - Ts), (0, 0))).reshape(Ms, -1)

    def unpad_s(a):
        return a.reshape(Bs, Tsp, -1)[:, :Ts].reshape(Bs * Ts, -1)

    xp = x_prompt.reshape(Mp, D)
    xs = x_sample.reshape(Bs * Ts, D)
    tm_p = 1024 if Mp % 1024 == 0 else Mp
    C = RET_CHUNK if T % RET_CHUNK == 0 else T

    moe = lambda xa, xb, i: _moe_ln(
        xa, xb, moe_w_router, moe_b_router, moe_w_gu, moe_b_gu, moe_w_down, moe_b_down,
        ln_ffn_g[i], ln_ffn_b[i], i, 256, 1024)

    w_in, w_o = ret_w_in[0], ret_w_o[0]
    hp = _matmul(xp, w_in, 0, w_in.shape[1], tm_p, 512)
    op, ret_p = _retention(hp, jnp.zeros((B,) + state_ret.shape[2:], F32), 0, B, T, C, C,
                           ret_gn_g[0], ret_gn_b[0])
    mp = _matmul(op, w_o, 0, D, 512, 512)
    xp = _res_ln(xp, mp, ln_mix_g[0], ln_mix_b[0], 256)

    xs_pad = pad_s(xs)
    hs = _matmul(xs_pad, w_in, 0, w_in.shape[1], Ms, 512)
    os_, ret_s = _retention(hs, state_ret[0], past_len, Bs, Ts, Ts, Tsp, ret_gn_g[0], ret_gn_b[0])
    ms = _matmul(os_, w_o, 0, D, Ms, 512)
    xs = unpad_s(_res_ln(xs_pad, ms, ln_mix_g[0], ln_mix_b[0], Ms))

    xp, xs = moe(xp, xs, 0)

    w_in, w_o = dsa_w_in[0], dsa_w_o[0]
    ap, k_p, v_p, ik_p = _dsa_prompt(xp, w_in, dsa_kidx_g[0], dsa_kidx_b[0], t5_bias, B, T)
    mp = _matmul(ap, w_o, 0, D, tm_p, 512)
    xp = _res_ln(xp, mp, ln_mix_g[1], ln_mix_b[1], 256)

    xs_pad = pad_s(xs)
    as_, k_s, v_s, ik_s = _dsa_sample(xs_pad, cache_k[0], cache_v[0], cache_kidx[0], page_table, w_in,
                                      dsa_kidx_g[0], dsa_kidx_b[0], t5_bias, Bs, Ts, Tsp, past_len)
    ms = _matmul(as_, w_o, 0, D, Ms, 512)
    xs = unpad_s(_res_ln(xs_pad, ms, ln_mix_g[1], ln_mix_b[1], Ms))

    xp, xs = moe(xp, xs, 1)

    return (xp.reshape(B, T, D), xs.reshape(Bs, Ts, D),
            ret_p[None], ret_s[None],
            k_p.reshape(1, B, T, KV_HEADS, ATT_DH), v_p.reshape(1, B, T, KV_HEADS, ATT_DH),
            ik_p.reshape(1, B, T, IDX_DH),
            unpad_s(k_s).reshape(1, Bs, Ts, KV_HEADS, ATT_DH),
            unpad_s(v_s).reshape(1, Bs, Ts, KV_HEADS, ATT_DH),
            unpad_s(ik_s).reshape(1, Bs, Ts, IDX_DH))
```

```python
import functools
import math

import numpy as np
import jax
import jax.numpy as jnp
from jax import lax
from jax.experimental import pallas as pl
from jax.experimental.pallas import tpu as pltpu

F32 = jnp.float32
BF16 = jnp.bfloat16
I32 = jnp.int32

RET_HEADS = 8
RET_HB = 4
RET_CHUNK = 128
ROPE_BASE = 10000.0
ATT_HEADS = 16
ATT_DH = 128
KV_HEADS = 4
ATT_GROUP = ATT_HEADS // KV_HEADS
IDX_HEADS = 16
IDX_DH = 128
TOPK_MAX = 256
T5_BUCKETS = 32
T5_MAX_DIST = 128
N_EXPERTS = 32
TOP_K = 4
SWIGLU_LIMIT = 7.0
SWIGLU_ALPHA = 1.702
DEPTH = 2
DEEPNORM_ALPHA = (2.0 * DEPTH) ** 0.25
LN_EPS = 1e-5

LANES = 128
SUBLANES = 8
VMEM_LIMIT = 56 * 1024 * 1024
NEG_BIG = -1e30
INT_MIN = -(2 ** 31)


def _cparams(sem):
    return pltpu.CompilerParams(dimension_semantics=sem, vmem_limit_bytes=VMEM_LIMIT)


def _ln_rows(x, g, b):
    mu = jnp.mean(x, axis=-1, keepdims=True)
    xc = x - mu
    var = jnp.mean(xc * xc, axis=-1, keepdims=True)
    return xc * lax.rsqrt(var + LN_EPS) * g + b


def _bf16_round(a):
    return a.astype(BF16).astype(F32)


def _dot(a, b):
    return jnp.dot(a, b, preferred_element_type=F32)


def _dot_nt(a, b):
    return lax.dot_general(a, b, (((1,), (1,)), ((), ())), preferred_element_type=F32)


def _mm_kernel(x_ref, w_ref, o_ref, xb_ref):
    @pl.when(pl.program_id(1) == 0)
    def _():
        xb_ref[...] = x_ref[...].astype(BF16)

    o_ref[...] = _dot(xb_ref[...], w_ref[...].astype(BF16)).astype(o_ref.dtype)


def _matmul(x, w, col0, ncols, tm, tn):
    M, K = x.shape
    assert M % tm == 0 and col0 % tn == 0 and ncols % tn == 0
    c0 = col0 // tn
    return pl.pallas_call(
        _mm_kernel,
        grid=(M // tm, ncols // tn),
        in_specs=[pl.BlockSpec((tm, K), lambda i, j: (i, 0)),
                  pl.BlockSpec((K, tn), lambda i, j: (0, j + c0))],
        out_specs=pl.BlockSpec((tm, tn), lambda i, j: (i, j)),
        out_shape=jax.ShapeDtypeStruct((M, ncols), F32),
        scratch_shapes=[pltpu.VMEM((tm, K), BF16)],
        compiler_params=_cparams(("parallel", "arbitrary")),
    )(x, w)


def _res_ln_kernel(x_ref, h_ref, g_ref, b_ref, o_ref, *, alpha):
    o_ref[...] = _ln_rows(alpha * x_ref[...] + h_ref[...], g_ref[...], b_ref[...])


def _ln_kernel(x_ref, g_ref, b_ref, o_ref):
    o_ref[...] = _ln_rows(x_ref[...], g_ref[...], b_ref[...])


def _res_ln(x, h, g, b, tm):
    M, D = x.shape
    row = pl.BlockSpec((tm, D), lambda i: (i, 0))
    par = pl.BlockSpec((1, D), lambda i: (0, 0))
    return pl.pallas_call(
        functools.partial(_res_ln_kernel, alpha=DEEPNORM_ALPHA),
        grid=(M // tm,),
        in_specs=[row, row, par, par],
        out_specs=row,
        out_shape=jax.ShapeDtypeStruct((M, D), F32),
        compiler_params=_cparams(("parallel",)),
    )(x, h, g.reshape(1, D), b.reshape(1, D))


def _ln_cols(x, col_blk, g, b, tm):
    M = x.shape[0]
    D = LANES
    par = pl.BlockSpec((1, D), lambda i: (0, 0))
    return pl.pallas_call(
        _ln_kernel,
        grid=(M // tm,),
        in_specs=[pl.BlockSpec((tm, D), lambda i: (i, col_blk)), par, par],
        out_specs=pl.BlockSpec((tm, D), lambda i: (i, 0)),
        out_shape=jax.ShapeDtypeStruct((M, D), F32),
        compiler_params=_cparams(("parallel",)),
    )(x, g.reshape(1, D), b.reshape(1, D))


def _ret_kernel(q_ref, k_ref, v_ref, g_ref, cos_ref, sin_ref, intra_ref, qd_ref, kd_ref, sd_ref,
                s0_ref, gng_ref, gnb_ref, o_ref, so_ref, st_ref, *, dk):
    c = pl.program_id(2)
    half = dk // 2
    dv = v_ref.shape[1] // RET_HB

    @pl.when(c == 0)
    def _():
        st_ref[...] = s0_ref[0]

    cos = cos_ref[...]
    sin = sin_ref[...]

    def rot(x):
        x1, x2 = x[:, :half], x[:, half:]
        return jnp.concatenate([x1 * cos - x2 * sin, x1 * sin + x2 * cos], axis=-1)

    for j in range(RET_HB):
        q = rot(q_ref[:, j * dk:(j + 1) * dk])
        k = rot(k_ref[:, j * dk:(j + 1) * dk]) * (dk ** -0.5)
        vb = v_ref[:, j * dv:(j + 1) * dv].astype(BF16)
        qb = q.astype(BF16)
        st = st_ref[j]
        scores = _dot_nt(qb, k.astype(BF16)) * intra_ref[j]
        o = _dot(scores.astype(BF16), vb)
        o = o + _dot(qb, st.astype(BF16)) * qd_ref[j]
        kd = (k * kd_ref[j]).astype(BF16)
        new_st = st * sd_ref[j] + lax.dot_general(kd, vb, (((0,), (0,)), ((), ())),
                                                  preferred_element_type=F32)
        st_ref[j] = new_st
        so_ref[0, j] = new_st
        y = _ln_rows(o, gng_ref[j], gnb_ref[j])
        g = g_ref[:, j * dv:(j + 1) * dv]
        o_ref[:, j * dv:(j + 1) * dv] = y * (g * (1.0 / (1.0 + jnp.exp(-g))))


def _ret_log_decay():
    return jnp.log1p(-jnp.exp2(-5.0 - jnp.arange(RET_HEADS, dtype=F32)))


def _retention(h, state0, pos0, B, T, C, Cp, gn_g, gn_b):
    H = RET_HEADS
    DK, DV = state0.shape[2], state0.shape[3]
    NC = T // C
    Tp = NC * Cp
    half = DK // 2
    log_g = _ret_log_decay()
    pos = (pos0 + jnp.arange(Tp, dtype=jnp.int32)).astype(F32)
    inv = ROPE_BASE ** (-jnp.arange(half, dtype=F32) / half)
    ang = pos[:, None] * inv[None, :]
    cos, sin = jnp.cos(ang), jnp.sin(ang)
    p = jnp.arange(C, dtype=F32)
    rel = p[:, None] - p[None, :]
    intra = jnp.where(rel[None] >= 0, jnp.exp(log_g[:, None, None] * jnp.maximum(rel, 0.0)[None]), 0.0)
    qd = jnp.exp((p[:, None] + 1.0) * log_g[None, :]).T[:, :, None]
    kd = jnp.exp((C - 1.0 - p)[:, None] * log_g[None, :]).T[:, :, None]
    sd = jnp.exp(C * log_g)[:, None, None]
    padc = Cp - C
    intra = jnp.pad(intra, ((0, 0), (0, padc), (0, padc)))
    qd = jnp.pad(qd, ((0, 0), (0, padc), (0, 0)))
    kd = jnp.pad(kd, ((0, 0), (0, padc), (0, 0)))

    HB = RET_HB
    assert H % HB == 0
    kb = H // HB
    vb = (2 * H * DK) // (HB * DV)
    gb = vb + H // HB
    assert vb * HB * DV == 2 * H * DK
    row = lambda b, hg, c: b * NC + c
    in_specs = [
        pl.BlockSpec((Cp, HB * DK), lambda b, hg, c: (row(b, hg, c), hg)),
        pl.BlockSpec((Cp, HB * DK), lambda b, hg, c: (row(b, hg, c), kb + hg)),
        pl.BlockSpec((Cp, HB * DV), lambda b, hg, c: (row(b, hg, c), vb + hg)),
        pl.BlockSpec((Cp, HB * DV), lambda b, hg, c: (row(b, hg, c), gb + hg)),
        pl.BlockSpec((Cp, half), lambda b, hg, c: (c, 0)),
        pl.BlockSpec((Cp, half), lambda b, hg, c: (c, 0)),
        pl.BlockSpec((HB, Cp, Cp), lambda b, hg, c: (hg, 0, 0)),
        pl.BlockSpec((HB, Cp, 1), lambda b, hg, c: (hg, 0, 0)),
        pl.BlockSpec((HB, Cp, 1), lambda b, hg, c: (hg, 0, 0)),
        pl.BlockSpec((HB, 1, 1), lambda b, hg, c: (hg, 0, 0)),
        pl.BlockSpec((1, HB, DK, DV), lambda b, hg, c: (b, hg, 0, 0)),
        pl.BlockSpec((HB, 1, DV), lambda b, hg, c: (hg, 0, 0)),
        pl.BlockSpec((HB, 1, DV), lambda b, hg, c: (hg, 0, 0)),
    ]
    out_specs = [
        pl.BlockSpec((Cp, HB * DV), lambda b, hg, c: (row(b, hg, c), hg)),
        pl.BlockSpec((1, HB, DK, DV), lambda b, hg, c: (b, hg, 0, 0)),
    ]
    o, st = pl.pallas_call(
        functools.partial(_ret_kernel, dk=DK),
        grid=(B, H // HB, NC),
        in_specs=in_specs,
        out_specs=out_specs,
        out_shape=[jax.ShapeDtypeStruct((B * Tp, H * DV), F32),
                   jax.ShapeDtypeStruct((B, H, DK, DV), F32)],
        scratch_shapes=[pltpu.VMEM((HB, DK, DV), F32)],
        compiler_params=_cparams(("parallel", "parallel", "arbitrary")),
    )(h, h, h, h, cos, sin, intra, qd, kd, sd, state0,
      gn_g.reshape(H, 1, DV), gn_b.reshape(H, 1, DV))
    return o, st


IDX_SCALE = IDX_HEADS ** -0.5 * IDX_DH ** -0.5


KEY_SB = 4 * LANES


def _select_rows(sc, qpos, topk, key_ref, o_ref, nsb=None):
    tr, S = sc.shape
    col = lax.broadcasted_iota(I32, (tr, S), 1)
    valid = col <= qpos
    bits = pltpu.bitcast(sc, I32)
    key = jnp.where(sc == 0.0, 0, bits ^ ((bits >> 31) & 0x7FFFFFFF))
    key_ref[...] = jnp.where(valid, key, INT_MIN)

    if nsb is None:
        nblk = S // LANES

        def count(pred):
            return jnp.sum(jnp.where(pred(key_ref[...]), 1.0, 0.0), axis=-1, keepdims=True)
    else:
        nblk = nsb * (KEY_SB // LANES)

        def count(pred):
            def sb_step(sb, acc):
                off = pl.multiple_of(sb * KEY_SB, KEY_SB)
                w = jnp.where(pred(key_ref[:, pl.ds(off, KEY_SB)]), 1.0, 0.0)
                for j in range(KEY_SB // LANES):
                    acc = acc + w[:, j * LANES:(j + 1) * LANES]
                return acc

            acc = lax.fori_loop(0, nsb, sb_step, jnp.zeros((tr, LANES), F32))
            return jnp.sum(acc, axis=-1, keepdims=True)

    def bit_step(i, ans):
        cand = ans | jnp.left_shift(jnp.int32(1), 31 - i)
        cand_s = cand ^ INT_MIN
        return jnp.where(count(lambda k: k >= cand_s) >= topk, cand, ans)

    ans = lax.fori_loop(0, 32, bit_step, jnp.zeros((tr, 1), I32))
    thr = ans ^ INT_MIN
    need = topk - count(lambda k: k > thr)

    r_i = lax.broadcasted_iota(I32, (LANES, LANES), 0)
    c_i = lax.broadcasted_iota(I32, (LANES, LANES), 1)
    upper = jnp.where(r_i < c_i, 1.0, 0.0).astype(BF16)

    def blk_step(j, carry):
        off = pl.multiple_of(j * LANES, LANES)
        kb = key_ref[:, pl.ds(off, LANES)]
        eq = jnp.where(kb == thr, jnp.where(kb == INT_MIN, 0.0, 1.0), 0.0)
        pre = _dot(eq.astype(BF16), upper) + carry
        take = jnp.where(pre < need, eq, 0.0)
        sel = jnp.where(kb > thr, 1.0, take)
        o_ref[:, pl.ds(off, LANES)] = jnp.where(sel > 0.5, 0.0, NEG_BIG)
        return carry + jnp.sum(eq, axis=-1, keepdims=True)

    lax.fori_loop(0, nblk, blk_step, jnp.zeros((tr, 1), F32))


def _select_kernel(sc_ref, qpos_ref, o_ref, key_ref, *, topk):
    _select_rows(sc_ref[...], qpos_ref[...], topk, key_ref, o_ref)


def _select(scores, qpos, topk, tr):
    R, S = scores.shape
    return pl.pallas_call(
        functools.partial(_select_kernel, topk=topk),
        grid=(R // tr,),
        in_specs=[pl.BlockSpec((tr, S), lambda i: (i, 0)),
                  pl.BlockSpec((tr, 1), lambda i: (i, 0))],
        out_specs=pl.BlockSpec((tr, S), lambda i: (i, 0)),
        out_shape=jax.ShapeDtypeStruct((R, S), F32),
        scratch_shapes=[pltpu.VMEM((tr, S), I32)],
        compiler_params=_cparams(("parallel",)),
    )(scores, qpos)


def _t5_bucket(dist):
    max_exact = T5_BUCKETS // 2
    is_small = dist < max_exact
    d = jnp.maximum(dist, 1).astype(F32)
    large = max_exact + (jnp.log(d / max_exact) / math.log(T5_MAX_DIST / max_exact)
                         * (T5_BUCKETS - max_exact)).astype(jnp.int32)
    large = jnp.minimum(large, T5_BUCKETS - 1)
    return jnp.where(is_small, dist, large)


def _bias_table(t5_bias, tq, tk):
    i = jnp.arange(tq, dtype=jnp.int32)[:, None]
    j = jnp.arange(tk, dtype=jnp.int32)[None, :]
    tabs = []
    buckets = jnp.arange(T5_BUCKETS, dtype=jnp.int32)
    for c in range(3):
        dist = jnp.maximum(c * tk + i - j, 0)
        onehot = (_t5_bucket(dist)[..., None] == buckets).astype(F32)
        tabs.append(jnp.einsum('ijb,bh->hij', onehot, t5_bias, precision=lax.Precision.HIGHEST))
    return jnp.stack(tabs).astype(F32).reshape(3, KV_HEADS, ATT_GROUP * tq, tk)


def _far_bucket_ok(tk):
    d = np.arange(tk + 1, 8 * tk + 2).astype(np.float32)
    me = T5_BUCKETS // 2
    large = me + (np.log(d / me) / math.log(T5_MAX_DIST / me) * (T5_BUCKETS - me)).astype(np.int32)
    return bool(np.all(np.minimum(large, T5_BUCKETS - 1) == T5_BUCKETS - 1))


Q_W = ATT_HEADS * ATT_DH
KV_W = KV_HEADS * ATT_DH
IQ_W = IDX_HEADS * IDX_DH
DSA_MAIN = Q_W + 2 * KV_W + IQ_W
ATT_SCALE = ATT_DH ** -0.5


def _dsa_project(x, w_in, kidx_g, kidx_b, tm):
    h_main = _matmul(x, w_in, 0, DSA_MAIN, tm, 512)
    tail = w_in.shape[1] - DSA_MAIN
    w_tail = jnp.pad(w_in[:, DSA_MAIN:], ((0, 0), (0, 2 * LANES - tail)))
    ikw = _matmul(x, w_tail, 0, 2 * LANES, tm, 2 * LANES)
    ik = _ln_cols(ikw, 0, kidx_g, kidx_b, min(tm, 512))
    return h_main, ik, ikw


def _dsap_kernel(iq0_ref, iq1_ref, iw_ref, ik_ref, q_ref, k_ref, v_ref, tab_ref, o_ref,
                 iqs_ref, iwb_ref, sc_ref, key_ref, mb_ref, qs_ref, s_ref, m_ref, l_ref, acc_ref,
                 *, topk):
    qi = pl.program_id(1)
    tq = q_ref.shape[0]
    half = IDX_HEADS // 2
    nj = KEY_SB // LANES
    nsb = (qi + nj) // nj

    iw = iw_ref[...]
    for hh in range(IDX_HEADS):
        src = iq0_ref if hh < half else iq1_ref
        j = hh % half
        iqs_ref[hh * tq:(hh + 1) * tq, :] = src[:, j * IDX_DH:(j + 1) * IDX_DH].astype(BF16)
        iwb_ref[hh * tq:(hh + 1) * tq, :] = jnp.broadcast_to(
            _bf16_round(iw[:, hh:hh + 1] * IDX_SCALE), (tq, LANES))

    sc_ref[...] = jnp.zeros_like(sc_ref)

    def sc_sb(sb, c):
        for j in range(nj):
            off = pl.multiple_of(sb * KEY_SB + j * LANES, LANES)
            ikb = ik_ref[pl.ds(off, LANES), :].astype(BF16)
            r = _bf16_round(jnp.maximum(_dot_nt(iqs_ref[...], ikb), 0.0)) * iwb_ref[...]
            acc = r[0:tq]
            for hh in range(1, IDX_HEADS):
                acc = acc + r[hh * tq:(hh + 1) * tq]
            sc_ref[:, pl.ds(off, LANES)] = acc
        return c

    lax.fori_loop(0, nsb, sc_sb, 0)

    qpos = qi * tq + lax.broadcasted_iota(I32, (tq, 1), 0)
    _select_rows(sc_ref[...], qpos, topk, key_ref, mb_ref, nsb)

    for n in range(KV_HEADS):
        for g in range(ATT_GROUP):
            hh = n * ATT_GROUP + g
            qs_ref[g * tq:(g + 1) * tq, :] = q_ref[:, hh * ATT_DH:(hh + 1) * ATT_DH].astype(BF16)
        m_ref[...] = jnp.full_like(m_ref, -3e38)

        def stage_a(sb, c):
            off = pl.multiple_of(sb * KEY_SB, KEY_SB)
            kblk = k_ref[pl.ds(off, KEY_SB), n * ATT_DH:(n + 1) * ATT_DH].astype(BF16)
            s = _dot_nt(qs_ref[...], kblk) * ATT_SCALE
            m = m_ref[...]
            for j in range(nj):
                offj = pl.multiple_of(sb * KEY_SB + j * LANES, LANES)
                mbt = mb_ref[:, pl.ds(offj, LANES)]
                cls = jnp.clip(qi - (sb * nj + j), 0, 2)
                sj = (s[:, j * LANES:(j + 1) * LANES] + tab_ref[cls, n]
                      + jnp.concatenate([mbt] * ATT_GROUP, axis=0))
                s_ref[:, pl.ds(offj, LANES)] = sj
                m = jnp.maximum(m, sj)
            m_ref[...] = m
            return c

        lax.fori_loop(0, nsb, stage_a, 0)
        m_ref[...] = jnp.broadcast_to(jnp.max(m_ref[...], axis=-1, keepdims=True), m_ref.shape)
        l_ref[...] = jnp.zeros_like(l_ref)
        acc_ref[...] = jnp.zeros_like(acc_ref)

        def stage_b(sb, c):
            off = pl.multiple_of(sb * KEY_SB, KEY_SB)
            m = m_ref[...]
            l = l_ref[...]
            ps = []
            for j in range(nj):
                offj = pl.multiple_of(sb * KEY_SB + j * LANES, LANES)
                pj = jnp.exp(s_ref[:, pl.ds(offj, LANES)] - m)
                l = l + pj
                ps.append(pj.astype(BF16))
            l_ref[...] = l
            vblk = v_ref[pl.ds(off, KEY_SB), n * ATT_DH:(n + 1) * ATT_DH].astype(BF16)
            acc_ref[...] += _dot(jnp.concatenate(ps, axis=1), vblk)
            return c

        lax.fori_loop(0, nsb, stage_b, 0)
        o = acc_ref[...] / jnp.sum(l_ref[...], axis=-1, keepdims=True)
        for g in range(ATT_GROUP):
            hh = n * ATT_GROUP + g
            o_ref[:, hh * ATT_DH:(hh + 1) * ATT_DH] = o[g * tq:(g + 1) * tq]


def _dsa_prompt(x, w_in, kidx_g, kidx_b, t5_bias, B, T):
    M = B * T
    h_main, ik, ikw = _dsa_project(x, w_in, kidx_g, kidx_b, 1024 if M % 1024 == 0 else M)
    topk = min(TOPK_MAX, T // 4)
    tq = LANES
    assert _far_bucket_ok(LANES) and T % KEY_SB == 0
    nq = T // tq
    iqc = (Q_W + 2 * KV_W) // (IQ_W // 2)
    assert iqc * (IQ_W // 2) == Q_W + 2 * KV_W
    kcol = Q_W // KV_W
    gq = ATT_GROUP * tq
    o = pl.pallas_call(
        functools.partial(_dsap_kernel, topk=topk),
        grid=(B, nq),
        in_specs=[pl.BlockSpec((tq, IQ_W // 2), lambda b, qi: (b * nq + qi, iqc)),
                  pl.BlockSpec((tq, IQ_W // 2), lambda b, qi: (b * nq + qi, iqc + 1)),
                  pl.BlockSpec((tq, LANES), lambda b, qi: (b * nq + qi, 1)),
                  pl.BlockSpec((T, IDX_DH), lambda b, qi: (b, 0)),
                  pl.BlockSpec((tq, Q_W), lambda b, qi: (b * nq + qi, 0)),
                  pl.BlockSpec((T, KV_W), lambda b, qi: (b, kcol)),
                  pl.BlockSpec((T, KV_W), lambda b, qi: (b, kcol + 1)),
                  pl.BlockSpec((3, KV_HEADS, gq, LANES), lambda b, qi: (0, 0, 0, 0))],
        out_specs=pl.BlockSpec((tq, Q_W), lambda b, qi: (b * nq + qi, 0)),
        out_shape=jax.ShapeDtypeStruct((M, Q_W), F32),
        scratch_shapes=[pltpu.VMEM((IDX_HEADS * tq, IDX_DH), BF16),
                        pltpu.VMEM((IDX_HEADS * tq, LANES), F32),
                        pltpu.VMEM((tq, T), F32),
                        pltpu.VMEM((tq, T), I32),
                        pltpu.VMEM((tq, T), F32),
                        pltpu.VMEM((gq, ATT_DH), BF16),
                        pltpu.VMEM((gq, T), F32),
                        pltpu.VMEM((gq, LANES), F32),
                        pltpu.VMEM((gq, LANES), F32),
                        pltpu.VMEM((gq, ATT_DH), F32)],
        compiler_params=_cparams(("parallel", "arbitrary")),
    )(h_main, h_main, ikw, ik, h_main, h_main, h_main, _bias_table(t5_bias, tq, LANES))
    k_new = h_main[:, Q_W:Q_W + KV_W]
    v_new = h_main[:, Q_W + KV_W:Q_W + 2 * KV_W]
    return o, k_new, v_new, ik


PAGES_PER_STEP = 16


def _head_rows(ref, n):
    return ref[0, pl.ds(n, LANES, stride=KV_HEADS), :]


def _idxs_kernel(pt_ref, iq_ref, iw_ref, *refs, n_steps):
    G = PAGES_PER_STEP
    page_refs, kn_ref, o_ref = refs[:G], refs[G], refs[G + 1]
    step = pl.program_id(1)
    tp = o_ref.shape[0]
    iq = iq_ref[0].astype(BF16)
    iwb = _bf16_round(iw_ref[0] * IDX_SCALE)

    def scores(keys):
        r = _bf16_round(jnp.maximum(_dot_nt(iq, keys.astype(BF16)), 0.0)) * iwb
        acc = r[0:tp]
        for hh in range(1, IDX_HEADS):
            acc = acc + r[hh * tp:(hh + 1) * tp]
        return acc

    @pl.when(step < n_steps - 1)
    def _():
        for j in range(G):
            o_ref[:, j * LANES:(j + 1) * LANES] = scores(page_refs[j][0])

    @pl.when(step == n_steps - 1)
    def _():
        o_ref[...] = jnp.zeros_like(o_ref)
        o_ref[:, 0:LANES] = scores(kn_ref[0])


def _attns_kernel(pt_ref, q_ref, *refs, n_steps, n_pages):
    G = PAGES_PER_STEP
    kp, vp = refs[:G], refs[G:2 * G]
    kn_ref, vn_ref, mb_ref, tab_ref, o_ref, m_ref, l_ref, acc_ref = refs[2 * G:]
    step = pl.program_id(1)

    @pl.when(step == 0)
    def _():
        m_ref[...] = jnp.full_like(m_ref, -3e38)
        l_ref[...] = jnp.zeros_like(l_ref)
        acc_ref[...] = jnp.zeros_like(acc_ref)

    def process(pages):
        for n in range(KV_HEADS):
            qn = q_ref[0, n].astype(BF16)
            s_parts = []
            for (kr, vr, cls, cb) in pages:
                mbt = mb_ref[:, cb * LANES:(cb + 1) * LANES]
                s_parts.append(_dot_nt(qn, _head_rows(kr, n).astype(BF16)) * ATT_SCALE
                               + tab_ref[cls, n] + jnp.concatenate([mbt] * ATT_GROUP, axis=0))
            s = s_parts[0] if len(s_parts) == 1 else jnp.concatenate(s_parts, axis=1)
            m_old = m_ref[n]
            m_new = jnp.maximum(m_old, jnp.max(s, axis=-1, keepdims=True))
            a = jnp.exp(m_old - m_new)
            p = jnp.exp(s - m_new)
            l_ref[n] = a * l_ref[n] + jnp.sum(p, axis=-1, keepdims=True)
            pv = None
            for j, (kr, vr, cls, cb) in enumerate(pages):
                d = _dot(p[:, j * LANES:(j + 1) * LANES].astype(BF16), _head_rows(vr, n).astype(BF16))
                pv = d if pv is None else pv + d
            acc_ref[n] = a * acc_ref[n] + pv
            m_ref[n] = m_new

    @pl.when(step < n_steps - 1)
    def _():
        process([(kp[j], vp[j], jnp.minimum(n_pages - (step * G + j), 2), j) for j in range(G)])

    @pl.when(step == n_steps - 1)
    def _():
        process([(kn_ref, vn_ref, 0, 0)])
        for n in range(KV_HEADS):
            o_ref[0, n] = acc_ref[n] / l_ref[n]


def _dsa_sample(x, cache_k, cache_v, cache_kidx, page_table, w_in, kidx_g, kidx_b, t5_bias,
                B, T, Tp, past_len):
    M = B * Tp
    G = PAGES_PER_STEP
    page = cache_kidx.shape[1]
    n_pages = past_len // page
    assert page == LANES and _far_bucket_ok(page) and n_pages % G == 0
    h_main, ik, ikw = _dsa_project(x, w_in, kidx_g, kidx_b, M)
    topk = min(TOPK_MAX, (past_len + T) // 4)
    n_steps = n_pages // G + 1
    S = n_steps * G * page
    n_pool = cache_kidx.shape[0]
    ck = cache_k.reshape(n_pool, page * KV_HEADS, ATT_DH)
    cv = cache_v.reshape(n_pool, page * KV_HEADS, ATT_DH)
    pt = page_table.reshape(-1).astype(jnp.int32)
    k_new = h_main[:, Q_W:Q_W + KV_W]
    v_new = h_main[:, Q_W + KV_W:Q_W + 2 * KV_W]

    def as_page(a, w):
        return jnp.pad(a.reshape(B, Tp, w), ((0, 0), (0, page - Tp), (0, 0))).reshape(B, -1, ATT_DH)

    iq = h_main[:, Q_W + 2 * KV_W:].reshape(B, Tp, IDX_HEADS, IDX_DH)
    iq_st = jnp.transpose(iq, (0, 2, 1, 3)).reshape(B, IDX_HEADS * Tp, IDX_DH)
    iw = ikw[:, LANES:LANES + IDX_HEADS].reshape(B, Tp, IDX_HEADS)
    iw_st = jnp.broadcast_to(jnp.transpose(iw, (0, 2, 1)).reshape(B, IDX_HEADS * Tp, 1),
                             (B, IDX_HEADS * Tp, LANES))

    def page_spec(rows, j):
        return pl.BlockSpec(
            (1, rows, ATT_DH),
            lambda b, s, p: (p[b * n_pages + jnp.minimum(s * G + j, n_pages - 1)], 0, 0))

    stacked = pl.BlockSpec((1, IDX_HEADS * Tp, LANES), lambda b, s, p: (b, 0, 0))
    scores = pl.pallas_call(
        functools.partial(_idxs_kernel, n_steps=n_steps),
        grid_spec=pltpu.PrefetchScalarGridSpec(
            num_scalar_prefetch=1,
            grid=(B, n_steps),
            in_specs=[stacked, stacked] + [page_spec(page, j) for j in range(G)]
                     + [pl.BlockSpec((1, page, IDX_DH), lambda b, s, p: (b, 0, 0))],
            out_specs=pl.BlockSpec((Tp, G * page), lambda b, s, p: (b, s)),
        ),
        out_shape=jax.ShapeDtypeStruct((M, S), F32),
        compiler_params=_cparams(("parallel", "arbitrary")),
    )(pt, iq_st, iw_st, *([cache_kidx] * G), as_page(ik, IDX_DH))
    qpos = jnp.tile(past_len + jnp.arange(Tp, dtype=jnp.int32), B)[:, None]
    mbias = _select(scores, qpos, topk, M)

    gq = ATT_GROUP * Tp
    q = h_main[:, :Q_W].reshape(B, Tp, KV_HEADS, ATT_GROUP, ATT_DH)
    q_st = jnp.transpose(q, (0, 2, 3, 1, 4)).reshape(B, KV_HEADS, gq, ATT_DH)
    qspec = pl.BlockSpec((1, KV_HEADS, gq, ATT_DH), lambda b, s, p: (b, 0, 0, 0))
    newspec = pl.BlockSpec((1, page * KV_HEADS, ATT_DH), lambda b, s, p: (b, 0, 0))
    o_st = pl.pallas_call(
        functools.partial(_attns_kernel, n_steps=n_steps, n_pages=n_pages),
        grid_spec=pltpu.PrefetchScalarGridSpec(
            num_scalar_prefetch=1,
            grid=(B, n_steps),
            in_specs=[qspec] + [page_spec(page * KV_HEADS, j) for j in range(G)] * 2
                     + [newspec, newspec,
                        pl.BlockSpec((Tp, G * page), lambda b, s, p: (b, s)),
                        pl.BlockSpec((3, KV_HEADS, gq, page), lambda b, s, p: (0, 0, 0, 0))],
            out_specs=qspec,
            scratch_shapes=[pltpu.VMEM((KV_HEADS, gq, 1), F32),
                            pltpu.VMEM((KV_HEADS, gq, 1), F32),
                            pltpu.VMEM((KV_HEADS, gq, ATT_DH), F32)],
        ),
        out_shape=jax.ShapeDtypeStruct((B, KV_HEADS, gq, ATT_DH), F32),
        compiler_params=_cparams(("parallel", "arbitrary")),
    )(pt, q_st, *([ck] * G), *([cv] * G), as_page(k_new, KV_W), as_page(v_new, KV_W), mbias,
      _bias_table(t5_bias, Tp, page))
    o = jnp.transpose(o_st.reshape(B, KV_HEADS, ATT_GROUP, Tp, ATT_DH), (0, 3, 1, 2, 4)).reshape(M, Q_W)
    return o, k_new, v_new, ik


def _router_kernel(x_ref, w_ref, b_ref, e_ref, g_ref, p_ref, cnt_ref, carry_ref):
    i = pl.program_id(0)
    tm = x_ref.shape[0]
    E = N_EXPERTS

    @pl.when(i == 0)
    def _():
        carry_ref[...] = jnp.zeros_like(carry_ref)

    logits = _dot(x_ref[...].astype(BF16), w_ref[...].astype(BF16)) + b_ref[...]
    lane = lax.broadcasted_iota(I32, (tm, E), 1).astype(F32)
    lane_o = lax.broadcasted_iota(I32, (tm, LANES), 1)
    vals, idxs = [], []
    cur = logits
    onehot = jnp.zeros((tm, E), F32)
    for _ in range(TOP_K):
        mx = jnp.max(cur, axis=-1, keepdims=True)
        ix = jnp.min(jnp.where(cur == mx, lane, float(E)), axis=-1, keepdims=True)
        hit = lane == ix
        onehot = jnp.where(hit, 1.0, onehot)
        cur = jnp.where(hit, -jnp.inf, cur)
        vals.append(mx)
        idxs.append(ix)
    ex = [jnp.exp(v - vals[0]) for v in vals]
    den = ex[0] + ex[1] + ex[2] + ex[3]
    r_i = lax.broadcasted_iota(I32, (tm, tm), 0)
    c_i = lax.broadcasted_iota(I32, (tm, tm), 1)
    lower = jnp.where(c_i < r_i, 1.0, 0.0).astype(BF16)
    prefix = _dot(lower, onehot.astype(BF16)) + carry_ref[...]
    e_out = jnp.zeros((tm, LANES), I32)
    g_out = jnp.zeros((tm, LANES), F32)
    p_out = jnp.zeros((tm, LANES), I32)
    for k in range(TOP_K):
        pos = jnp.sum(jnp.where(lane == idxs[k], prefix, 0.0), axis=-1, keepdims=True)
        e_out = jnp.where(lane_o == k, idxs[k].astype(I32), e_out)
        g_out = jnp.where(lane_o == k, ex[k] / den, g_out)
        p_out = jnp.where(lane_o == k, pos.astype(I32), p_out)
    e_ref[...] = e_out
    g_ref[...] = g_out
    p_ref[...] = p_out
    carry_ref[...] = carry_ref[...] + jnp.sum(onehot, axis=0, keepdims=True)
    cnt_ref[...] = carry_ref[...].astype(I32)


def _router(x, w_router, b_router, layer, tm):
    T, D = x.shape
    E = N_EXPERTS
    outs = pl.pallas_call(
        _router_kernel,
        grid=(T // tm,),
        in_specs=[pl.BlockSpec((tm, D), lambda i: (i, 0)),
                  pl.BlockSpec((None, D, E), lambda i: (layer, 0, 0)),
                  pl.BlockSpec((None, 1, E), lambda i: (layer, 0, 0))],
        out_specs=[pl.BlockSpec((tm, LANES), lambda i: (i, 0)),
                   pl.BlockSpec((tm, LANES), lambda i: (i, 0)),
                   pl.BlockSpec((tm, LANES), lambda i: (i, 0)),
                   pl.BlockSpec((1, E), lambda i: (0, 0))],
        out_shape=[jax.ShapeDtypeStruct((T, LANES), I32),
                   jax.ShapeDtypeStruct((T, LANES), F32),
                   jax.ShapeDtypeStruct((T, LANES), I32),
                   jax.ShapeDtypeStruct((1, E), I32)],
        scratch_shapes=[pltpu.VMEM((1, E), F32)],
        compiler_params=_cparams(("arbitrary",)),
    )(x, w_router, b_router.reshape(-1, 1, E))
    return outs


ISSUE_UNROLL = 8


def _row_copy(src, s_row, dst, d_row, sem):
    return pltpu.make_async_copy(src.at[pl.ds(s_row, 1)], dst.at[pl.ds(d_row, 1)], sem)


def _dispatch_kernel(nv_ref, vl_ref, tok_ref, x_hbm, o_ref, buf_ref, sem):
    i = pl.program_id(0)
    tm = o_ref.shape[0]
    n_valid = nv_ref[0]

    def issue(step, slot):
        base = vl_ref[step] * tm

        def body(r, c):
            _row_copy(x_hbm, tok_ref[base + r], buf_ref.at[slot], r, sem.at[slot]).start()
            return c

        lax.fori_loop(0, tm, body, 0, unroll=ISSUE_UNROLL)

    slot = i % 2

    @pl.when(jnp.logical_and(i == 0, n_valid > 0))
    def _():
        issue(0, 0)

    @pl.when(i + 1 < n_valid)
    def _():
        issue(i + 1, 1 - slot)

    @pl.when(i < n_valid)
    def _():
        pltpu.make_async_copy(x_hbm.at[pl.ds(0, tm)], buf_ref.at[slot], sem.at[slot]).wait()
        o_ref[...] = buf_ref[slot].astype(o_ref.dtype)

    @pl.when(i >= n_valid)
    def _():
        o_ref[...] = jnp.zeros_like(o_ref)


def _dispatch(x, row_tok, vlist, n_valid, tm):
    T, D = x.shape
    assert T >= tm
    n_rows = row_tok.shape[0]
    return pl.pallas_call(
        _dispatch_kernel,
        grid_spec=pltpu.PrefetchScalarGridSpec(
            num_scalar_prefetch=3,
            grid=(vlist.shape[0],),
            in_specs=[pl.BlockSpec(memory_space=pl.ANY)],
            out_specs=pl.BlockSpec((tm, D), lambda i, nv, vl, tok: (vl[i], 0)),
            scratch_shapes=[pltpu.VMEM((2, tm, D), x.dtype), pltpu.SemaphoreType.DMA((2,))],
        ),
        out_shape=jax.ShapeDtypeStruct((n_rows, D), BF16),
        compiler_params=_cparams(("arbitrary",)),
    )(n_valid, vlist, row_tok, x)


RUN_BLOCKS = 5
RUN_VARIANTS = (2, 4, 5)


def _for_run_variant(n, tm, body, o_ref):
    lo = 0
    for c in RUN_VARIANTS:
        @pl.when(jnp.logical_and(n > lo, n <= c))
        def _(c=c):
            body(c * tm)
            if c * tm < o_ref.shape[0]:
                o_ref[c * tm:, :] = jnp.zeros((o_ref.shape[0] - c * tm, o_ref.shape[1]), o_ref.dtype)
        lo = c

    @pl.when(n == 0)
    def _():
        o_ref[...] = jnp.zeros_like(o_ref)


def _gu_kernel(re_ref, rn_ref, rt_ref, x_ref, wg_ref, wu_ref, bg_ref, bu_ref, o_ref, *, tm):
    n = rn_ref[pl.program_id(1)]

    def body(m):
        xb = x_ref[0:m, :]
        gate = _dot(xb, wg_ref[...].astype(BF16)) + bg_ref[...]
        up = _dot(xb, wu_ref[...].astype(BF16)) + bu_ref[...]
        gate = jnp.minimum(gate, SWIGLU_LIMIT)
        up = jnp.clip(up, -SWIGLU_LIMIT, SWIGLU_LIMIT)
        sig = 1.0 / (1.0 + jnp.exp(-(gate * SWIGLU_ALPHA)))
        o_ref[0:m, :] = ((up + 1.0) * (gate * sig)).astype(o_ref.dtype)

    _for_run_variant(n, tm, body, o_ref)


def _down_kernel(re_ref, rn_ref, rt_ref, h_ref, w_ref, b_ref, o_ref, *, tm):
    n = rn_ref[pl.program_id(1)]

    def body(m):
        o_ref[0:m, :] = _dot(h_ref[0:m, :], w_ref[...].astype(BF16)) + b_ref[...]

    _for_run_variant(n, tm, body, o_ref)


def _experts(xs, run_e, run_n, run_total, w_gu, b_gu, w_down, b_down, layer, tm, tn_gu, tn_down):
    n_rows, D = xs.shape
    DE = w_down.shape[2]
    run = RUN_BLOCKS * tm
    NR = n_rows // run
    ng = DE // tn_gu
    rc = lambda r, rt: jnp.minimum(r, rt[0] - 1)
    hid = pl.pallas_call(
        functools.partial(_gu_kernel, tm=tm),
        grid_spec=pltpu.PrefetchScalarGridSpec(
            num_scalar_prefetch=3,
            grid=(ng, NR),
            in_specs=[pl.BlockSpec((run, D), lambda n, r, re, rn, rt: (rc(r, rt), 0)),
                      pl.BlockSpec((None, None, D, tn_gu), lambda n, r, re, rn, rt: (layer, re[r], 0, n)),
                      pl.BlockSpec((None, None, D, tn_gu), lambda n, r, re, rn, rt: (layer, re[r], 0, ng + n)),
                      pl.BlockSpec((None, None, 1, tn_gu), lambda n, r, re, rn, rt: (layer, re[r], 0, n)),
                      pl.BlockSpec((None, None, 1, tn_gu), lambda n, r, re, rn, rt: (layer, re[r], 0, ng + n))],
            out_specs=pl.BlockSpec((run, tn_gu), lambda n, r, re, rn, rt: (r, n)),
        ),
        out_shape=jax.ShapeDtypeStruct((n_rows, DE), BF16),
        compiler_params=_cparams(("arbitrary", "arbitrary")),
    )(run_e, run_n, run_total, xs, w_gu, w_gu, b_gu.reshape(DEPTH, N_EXPERTS, 1, 2 * DE),
      b_gu.reshape(DEPTH, N_EXPERTS, 1, 2 * DE))
    nd = D // tn_down
    out = pl.pallas_call(
        functools.partial(_down_kernel, tm=tm),
        grid_spec=pltpu.PrefetchScalarGridSpec(
            num_scalar_prefetch=3,
            grid=(nd, NR),
            in_specs=[pl.BlockSpec((run, DE), lambda n, r, re, rn, rt: (rc(r, rt), 0)),
                      pl.BlockSpec((None, None, DE, tn_down), lambda n, r, re, rn, rt: (layer, re[r], 0, n)),
                      pl.BlockSpec((None, None, 1, tn_down), lambda n, r, re, rn, rt: (layer, re[r], 0, n))],
            out_specs=pl.BlockSpec((run, tn_down), lambda n, r, re, rn, rt: (r, n)),
        ),
        out_shape=jax.ShapeDtypeStruct((n_rows, D), F32),
        compiler_params=_cparams(("arbitrary", "arbitrary")),
    )(run_e, run_n, run_total, hid, w_down, b_down.reshape(DEPTH, N_EXPERTS, 1, D))
    return out


def _combine_kernel(dest_ref, g_ref, x_ref, rows_hbm, lg_ref, lb_ref, o_ref, buf_ref, sem, *, tb):
    def issue(r, c):
        for k in range(TOP_K):
            _row_copy(rows_hbm, dest_ref[r * TOP_K + k], buf_ref.at[k], r, sem).start()
        return c

    lax.fori_loop(0, tb, issue, 0, unroll=ISSUE_UNROLL // TOP_K)
    for k in range(TOP_K):
        pltpu.make_async_copy(rows_hbm.at[pl.ds(0, tb)], buf_ref.at[k], sem).wait()
    g = _bf16_round(g_ref[...])
    y = _bf16_round(buf_ref[0]) * g[:, 0:1]
    for k in range(1, TOP_K):
        y = y + _bf16_round(buf_ref[k]) * g[:, k:k + 1]
    o_ref[...] = _ln_rows(DEEPNORM_ALPHA * x_ref[...] + y, lg_ref[...], lb_ref[...])


def _combine(x, rows, dest_flat, gates, ln_g, ln_b, tb):
    T, D = x.shape
    return pl.pallas_call(
        functools.partial(_combine_kernel, tb=tb),
        grid=(T // tb,),
        in_specs=[pl.BlockSpec((tb * TOP_K,), lambda i: (i,), memory_space=pltpu.SMEM),
                  pl.BlockSpec((tb, LANES), lambda i: (i, 0)),
                  pl.BlockSpec((tb, D), lambda i: (i, 0)),
                  pl.BlockSpec(memory_space=pl.ANY),
                  pl.BlockSpec((1, D), lambda i: (0, 0)),
                  pl.BlockSpec((1, D), lambda i: (0, 0))],
        out_specs=pl.BlockSpec((tb, D), lambda i: (i, 0)),
        out_shape=jax.ShapeDtypeStruct((T, D), F32),
        scratch_shapes=[pltpu.VMEM((TOP_K, tb, D), F32), pltpu.SemaphoreType.DMA(())],
        compiler_params=_cparams(("arbitrary",)),
    )(dest_flat, gates, x, rows, ln_g.reshape(1, D), ln_b.reshape(1, D))


def _moe_ln(xa, xb, w_router, b_router, w_gu, b_gu, w_down, b_down, ln_g, ln_b, layer, tm,
            tn_gu, tn_down):
    Ta, D = xa.shape
    Tb = xb.shape[0]
    E = N_EXPERTS
    tba = min(Ta, 256)
    ea, ga, pa, ca = _router(xa, w_router, b_router, layer, tba)
    eb, gb, pb, cb = _router(xb, w_router, b_router, layer, Tb)
    i32 = jnp.int32
    run = RUN_BLOCKS * tm
    ca, cb = ca[0], cb[0]
    counts = ca + cb
    runs_e = (counts + run - 1) // run
    run_end = jnp.cumsum(runs_e)
    run_start = run_end - runs_e
    total = run_end[-1]
    ea4, eb4 = ea[:, :TOP_K], eb[:, :TOP_K]

    def lookup(table, idx):
        hit = idx[..., None] == jnp.arange(E, dtype=i32)
        return jnp.sum(jnp.where(hit, table, 0), axis=-1)

    def count_le(sorted_vals, q):
        return jnp.sum((sorted_vals[None, :] <= q[:, None]).astype(i32), axis=1)

    dest_a = (lookup(run_start * run, ea4) + pa[:, :TOP_K]).reshape(-1).astype(i32)
    dest_b = (lookup(run_start * run + ca, eb4) + pb[:, :TOP_K]).reshape(-1).astype(i32)
    nr_max = E + -(-((Ta + Tb) * TOP_K) // run)
    r = jnp.arange(nr_max, dtype=i32)
    e_of_r = jnp.minimum(count_le(run_end, r), E - 1)
    rows_left = lookup(counts, e_of_r) - (r - lookup(run_start, e_of_r)) * run
    run_n = jnp.where(r < total, jnp.clip((rows_left + tm - 1) // tm, 0, RUN_BLOCKS), 0).astype(i32)
    last_e = jnp.sum(jnp.where(r == total - 1, e_of_r, 0))
    run_e = jnp.where(r < total, e_of_r, last_e).astype(i32)
    nblk = nr_max * RUN_BLOCKS
    b = jnp.arange(nblk, dtype=i32)
    valid = (b % RUN_BLOCKS) < jnp.repeat(run_n, RUN_BLOCKS)
    cum = jnp.cumsum(valid.astype(i32))
    n_valid = cum[-1]
    slot = jnp.where(valid, cum - 1, n_valid + b - cum)
    vlist = jnp.sum(jnp.where(slot[None, :] == b[:, None], b[None, :], 0), axis=1).astype(i32)
    tok = jnp.arange((Ta + Tb) * TOP_K, dtype=i32) // TOP_K
    row_tok = jnp.zeros((nblk * tm,), i32).at[jnp.concatenate([dest_a, dest_b])].set(tok)
    xs = _dispatch(jnp.concatenate([xa, xb], axis=0), row_tok, vlist, n_valid.reshape(1), tm)
    rows = _experts(xs, run_e, run_n, total.astype(i32).reshape(1), w_gu, b_gu, w_down, b_down,
                    layer, tm, tn_gu, tn_down)
    return (_combine(xa, rows, dest_a, ga, ln_g, ln_b, tba),
            _combine(xb, rows, dest_b, gb, ln_g, ln_b, Tb))


def kernel(x_prompt, x_sample, state_ret, cache_k, cache_v, cache_kidx, page_table, t5_bias, ret_w_in, ret_gn_g, ret_gn_b, ret_w_o, dsa_w_in, dsa_kidx_g, dsa_kidx_b, dsa_w_o, ln_mix_g, ln_mix_b, ln_ffn_g, ln_ffn_b, moe_w_router, moe_b_router, moe_w_gu, moe_b_gu, moe_w_down, moe_b_down):
    B, T, D = x_prompt.shape
    Bs, Ts, _ = x_sample.shape
    Tsp = SUBLANES
    past_len = page_table.shape[1] * cache_k.shape[2]
    Mp = B * T
    Ms = Bs * Tsp

    def pad_s(a):
        return jnp.pad(a.reshape(Bs, Ts, -1), ((0, 0), (0, Tsp - Ts), (0, 0))).reshape(Ms, -1)

    def unpad_s(a):
        return a.reshape(Bs, Tsp, -1)[:, :Ts].reshape(Bs * Ts, -1)

    xp = x_prompt.reshape(Mp, D)
    xs = x_sample.reshape(Bs * Ts, D)
    tm_p = 1024 if Mp % 1024 == 0 else Mp
    C = RET_CHUNK if T % RET_CHUNK == 0 else T

    moe = lambda xa, xb, i: _moe_ln(
        xa, xb, moe_w_router, moe_b_router, moe_w_gu, moe_b_gu, moe_w_down, moe_b_down,
        ln_ffn_g[i], ln_ffn_b[i], i, 256, 512, 1024)

    w_in, w_o = ret_w_in[0], ret_w_o[0]
    hp = _matmul(xp, w_in, 0, w_in.shape[1], tm_p, 512)
    op, ret_p = _retention(hp, jnp.zeros((B,) + state_ret.shape[2:], F32), 0, B, T, C, C,
                           ret_gn_g[0], ret_gn_b[0])
    mp = _matmul(op, w_o, 0, D, 512, 512)
    xp = _res_ln(xp, mp, ln_mix_g[0], ln_mix_b[0], 256)

    xs_pad = pad_s(xs)
    hs = _matmul(xs_pad, w_in, 0, w_in.shape[1], Ms, 512)
    os_, ret_s = _retention(hs, state_ret[0], past_len, Bs, Ts, Ts, Tsp, ret_gn_g[0], ret_gn_b[0])
    ms = _matmul(os_, w_o, 0, D, Ms, 512)
    xs = unpad_s(_res_ln(xs_pad, ms, ln_mix_g[0], ln_mix_b[0], Ms))

    xp, xs = moe(xp, xs, 0)

    w_in, w_o = dsa_w_in[0], dsa_w_o[0]
    ap, k_p, v_p, ik_p = _dsa_prompt(xp, w_in, dsa_kidx_g[0], dsa_kidx_b[0], t5_bias, B, T)
    mp = _matmul(ap, w_o, 0, D, tm_p, 512)
    xp = _res_ln(xp, mp, ln_mix_g[1], ln_mix_b[1], 256)

    xs_pad = pad_s(xs)
    as_, k_s, v_s, ik_s = _dsa_sample(xs_pad, cache_k[0], cache_v[0], cache_kidx[0], page_table, w_in,
                                      dsa_kidx_g[0], dsa_kidx_b[0], t5_bias, Bs, Ts, Tsp, past_len)
    ms = _matmul(as_, w_o, 0, D, Ms, 512)
    xs = unpad_s(_res_ln(xs_pad, ms, ln_mix_g[1], ln_mix_b[1], Ms))

    xp, xs = moe(xp, xs, 1)

    return (xp.reshape(B, T, D), xs.reshape(Bs, Ts, D),
            ret_p[None], ret_s[None],
            k_p.reshape(1, B, T, KV_HEADS, ATT_DH), v_p.reshape(1, B, T, KV_HEADS, ATT_DH),
            ik_p.reshape(1, B, T, IDX_DH),
            unpad_s(k_s).reshape(1, Bs, Ts, KV_HEADS, ATT_DH),
            unpad_s(v_s).reshape(1, Bs, Ts, KV_HEADS, ATT_DH),
            unpad_s(ik_s).reshape(1, Bs, Ts, IDX_DH))
```

```python
import functools
import math

import numpy as np
import jax
import jax.numpy as jnp
from jax import lax
from jax.experimental import pallas as pl
from jax.experimental.pallas import tpu as pltpu

F32 = jnp.float32
BF16 = jnp.bfloat16
I32 = jnp.int32

RET_HEADS = 8
RET_HB = 8
RET_CHUNK = 128
ROPE_BASE = 10000.0
ATT_HEADS = 16
ATT_DH = 128
KV_HEADS = 4
ATT_GROUP = ATT_HEADS // KV_HEADS
IDX_HEADS = 16
IDX_DH = 128
TOPK_MAX = 256
T5_BUCKETS = 32
T5_MAX_DIST = 128
N_EXPERTS = 32
TOP_K = 4
SWIGLU_LIMIT = 7.0
SWIGLU_ALPHA = 1.702
DEPTH = 2
DEEPNORM_ALPHA = (2.0 * DEPTH) ** 0.25
LN_EPS = 1e-5

LANES = 128
SUBLANES = 8
VMEM_LIMIT = 56 * 1024 * 1024
NEG_BIG = -1e30
INT_MIN = -(2 ** 31)


def _cparams(sem):
    return pltpu.CompilerParams(dimension_semantics=sem, vmem_limit_bytes=VMEM_LIMIT)


def _ln_rows(x, g, b):
    mu = jnp.mean(x, axis=-1, keepdims=True)
    xc = x - mu
    var = jnp.mean(xc * xc, axis=-1, keepdims=True)
    return xc * lax.rsqrt(var + LN_EPS) * g + b


def _bf16_round(a):
    return a.astype(BF16).astype(F32)


def _dot(a, b):
    return jnp.dot(a, b, preferred_element_type=F32)


def _dot_nt(a, b):
    return lax.dot_general(a, b, (((1,), (1,)), ((), ())), preferred_element_type=F32)


def _mm_kernel(x_ref, w_ref, o_ref, xb_ref):
    @pl.when(pl.program_id(1) == 0)
    def _():
        xb_ref[...] = x_ref[...].astype(BF16)

    o_ref[...] = _dot(xb_ref[...], w_ref[...].astype(BF16)).astype(o_ref.dtype)


def _mm_bf16_kernel(x_ref, w_ref, o_ref):
    o_ref[...] = _dot(x_ref[...], w_ref[...].astype(BF16)).astype(o_ref.dtype)


def _matmul(x, w, col0, ncols, tm, tn):
    M, K = x.shape
    assert M % tm == 0 and col0 % tn == 0 and ncols % tn == 0
    c0 = col0 // tn
    pre_rounded = x.dtype == BF16
    return pl.pallas_call(
        _mm_bf16_kernel if pre_rounded else _mm_kernel,
        grid=(M // tm, ncols // tn),
        in_specs=[pl.BlockSpec((tm, K), lambda i, j: (i, 0)),
                  pl.BlockSpec((K, tn), lambda i, j: (0, j + c0))],
        out_specs=pl.BlockSpec((tm, tn), lambda i, j: (i, j)),
        out_shape=jax.ShapeDtypeStruct((M, ncols), F32),
        scratch_shapes=[] if pre_rounded else [pltpu.VMEM((tm, K), BF16)],
        compiler_params=_cparams(("parallel", "arbitrary")),
    )(x, w)


def _res_ln_kernel(x_ref, h_ref, g_ref, b_ref, o_ref, *, alpha):
    o_ref[...] = _ln_rows(alpha * x_ref[...] + h_ref[...], g_ref[...], b_ref[...])


def _ln_kernel(x_ref, g_ref, b_ref, o_ref):
    o_ref[...] = _ln_rows(x_ref[...], g_ref[...], b_ref[...])


def _res_ln(x, h, g, b, tm):
    M, D = x.shape
    row = pl.BlockSpec((tm, D), lambda i: (i, 0))
    par = pl.BlockSpec((1, D), lambda i: (0, 0))
    return pl.pallas_call(
        functools.partial(_res_ln_kernel, alpha=DEEPNORM_ALPHA),
        grid=(M // tm,),
        in_specs=[row, row, par, par],
        out_specs=row,
        out_shape=jax.ShapeDtypeStruct((M, D), F32),
        compiler_params=_cparams(("parallel",)),
    )(x, h, g.reshape(1, D), b.reshape(1, D))


def _ln_cols(x, col_blk, g, b, tm):
    M = x.shape[0]
    D = LANES
    par = pl.BlockSpec((1, D), lambda i: (0, 0))
    return pl.pallas_call(
        _ln_kernel,
        grid=(M // tm,),
        in_specs=[pl.BlockSpec((tm, D), lambda i: (i, col_blk)), par, par],
        out_specs=pl.BlockSpec((tm, D), lambda i: (i, 0)),
        out_shape=jax.ShapeDtypeStruct((M, D), F32),
        compiler_params=_cparams(("parallel",)),
    )(x, g.reshape(1, D), b.reshape(1, D))


def _ret_kernel(q_ref, k_ref, v_ref, g_ref, cos_ref, sin_ref, intra_ref, qd_ref, kd_ref, sd_ref,
                s0_ref, gng_ref, gnb_ref, o_ref, so_ref, st_ref, *, dk):
    c = pl.program_id(2)
    half = dk // 2
    dv = v_ref.shape[1] // RET_HB

    @pl.when(c == 0)
    def _():
        st_ref[...] = s0_ref[0]

    cos = cos_ref[...]
    sin = sin_ref[...]

    def rot(x):
        x1, x2 = x[:, :half], x[:, half:]
        return jnp.concatenate([x1 * cos - x2 * sin, x1 * sin + x2 * cos], axis=-1)

    for j in range(RET_HB):
        q = rot(q_ref[:, j * dk:(j + 1) * dk])
        k = rot(k_ref[:, j * dk:(j + 1) * dk]) * (dk ** -0.5)
        vb = v_ref[:, j * dv:(j + 1) * dv].astype(BF16)
        qb = q.astype(BF16)
        st = st_ref[j]
        scores = _dot_nt(qb, k.astype(BF16)) * intra_ref[j]
        o = _dot(scores.astype(BF16), vb)
        o = o + _dot(qb, st.astype(BF16)) * qd_ref[j]
        kd = (k * kd_ref[j]).astype(BF16)
        new_st = st * sd_ref[j] + lax.dot_general(kd, vb, (((0,), (0,)), ((), ())),
                                                  preferred_element_type=F32)
        st_ref[j] = new_st
        so_ref[0, j] = new_st
        y = _ln_rows(o, gng_ref[j], gnb_ref[j])
        g = g_ref[:, j * dv:(j + 1) * dv]
        o_ref[:, j * dv:(j + 1) * dv] = (y * (g * (1.0 / (1.0 + jnp.exp(-g))))).astype(o_ref.dtype)


def _ret_log_decay():
    return jnp.log1p(-jnp.exp2(-5.0 - jnp.arange(RET_HEADS, dtype=F32)))


def _retention(h, state0, pos0, B, T, C, Cp, gn_g, gn_b, out_dtype=F32):
    H = RET_HEADS
    DK, DV = state0.shape[2], state0.shape[3]
    NC = T // C
    Tp = NC * Cp
    half = DK // 2
    log_g = _ret_log_decay()
    pos = (pos0 + jnp.arange(Tp, dtype=jnp.int32)).astype(F32)
    inv = ROPE_BASE ** (-jnp.arange(half, dtype=F32) / half)
    ang = pos[:, None] * inv[None, :]
    cos, sin = jnp.cos(ang), jnp.sin(ang)
    p = jnp.arange(C, dtype=F32)
    rel = p[:, None] - p[None, :]
    intra = jnp.where(rel[None] >= 0, jnp.exp(log_g[:, None, None] * jnp.maximum(rel, 0.0)[None]), 0.0)
    qd = jnp.exp((p[:, None] + 1.0) * log_g[None, :]).T[:, :, None]
    kd = jnp.exp((C - 1.0 - p)[:, None] * log_g[None, :]).T[:, :, None]
    sd = jnp.exp(C * log_g)[:, None, None]
    padc = Cp - C
    intra = jnp.pad(intra, ((0, 0), (0, padc), (0, padc)))
    qd = jnp.pad(qd, ((0, 0), (0, padc), (0, 0)))
    kd = jnp.pad(kd, ((0, 0), (0, padc), (0, 0)))

    HB = RET_HB
    assert H % HB == 0
    kb = H // HB
    vb = (2 * H * DK) // (HB * DV)
    gb = vb + H // HB
    assert vb * HB * DV == 2 * H * DK
    row = lambda b, hg, c: b * NC + c
    in_specs = [
        pl.BlockSpec((Cp, HB * DK), lambda b, hg, c: (row(b, hg, c), hg)),
        pl.BlockSpec((Cp, HB * DK), lambda b, hg, c: (row(b, hg, c), kb + hg)),
        pl.BlockSpec((Cp, HB * DV), lambda b, hg, c: (row(b, hg, c), vb + hg)),
        pl.BlockSpec((Cp, HB * DV), lambda b, hg, c: (row(b, hg, c), gb + hg)),
        pl.BlockSpec((Cp, half), lambda b, hg, c: (c, 0)),
        pl.BlockSpec((Cp, half), lambda b, hg, c: (c, 0)),
        pl.BlockSpec((HB, Cp, Cp), lambda b, hg, c: (hg, 0, 0)),
        pl.BlockSpec((HB, Cp, 1), lambda b, hg, c: (hg, 0, 0)),
        pl.BlockSpec((HB, Cp, 1), lambda b, hg, c: (hg, 0, 0)),
        pl.BlockSpec((HB, 1, 1), lambda b, hg, c: (hg, 0, 0)),
        pl.BlockSpec((1, HB, DK, DV), lambda b, hg, c: (b, hg, 0, 0)),
        pl.BlockSpec((HB, 1, DV), lambda b, hg, c: (hg, 0, 0)),
        pl.BlockSpec((HB, 1, DV), lambda b, hg, c: (hg, 0, 0)),
    ]
    out_specs = [
        pl.BlockSpec((Cp, HB * DV), lambda b, hg, c: (row(b, hg, c), hg)),
        pl.BlockSpec((1, HB, DK, DV), lambda b, hg, c: (b, hg, 0, 0)),
    ]
    o, st = pl.pallas_call(
        functools.partial(_ret_kernel, dk=DK),
        grid=(B, H // HB, NC),
        in_specs=in_specs,
        out_specs=out_specs,
        out_shape=[jax.ShapeDtypeStruct((B * Tp, H * DV), out_dtype),
                   jax.ShapeDtypeStruct((B, H, DK, DV), F32)],
        scratch_shapes=[pltpu.VMEM((HB, DK, DV), F32)],
        compiler_params=_cparams(("parallel", "parallel", "arbitrary")),
    )(h, h, h, h, cos, sin, intra, qd, kd, sd, state0,
      gn_g.reshape(H, 1, DV), gn_b.reshape(H, 1, DV))
    return o, st


IDX_SCALE = IDX_HEADS ** -0.5 * IDX_DH ** -0.5


KEY_SB = 4 * LANES


def _select_rows(sc, qpos, topk, key_ref, o_ref, nsb=None):
    tr, S = sc.shape
    col = lax.broadcasted_iota(I32, (tr, S), 1)
    valid = col <= qpos
    bits = pltpu.bitcast(sc, I32)
    key = jnp.where(sc == 0.0, 0, bits ^ ((bits >> 31) & 0x7FFFFFFF))
    key_ref[...] = jnp.where(valid, key, INT_MIN)

    if nsb is None:
        nblk = S // LANES

        def count(pred):
            return jnp.sum(jnp.where(pred(key_ref[...]), 1.0, 0.0), axis=-1, keepdims=True)
    else:
        nblk = nsb * (KEY_SB // LANES)

        def count(pred):
            def sb_step(sb, acc):
                off = pl.multiple_of(sb * KEY_SB, KEY_SB)
                w = jnp.where(pred(key_ref[:, pl.ds(off, KEY_SB)]), 1.0, 0.0)
                for j in range(KEY_SB // LANES):
                    acc = acc + w[:, j * LANES:(j + 1) * LANES]
                return acc

            acc = lax.fori_loop(0, nsb, sb_step, jnp.zeros((tr, LANES), F32))
            return jnp.sum(acc, axis=-1, keepdims=True)

    def bit_step(i, ans):
        cand = ans | jnp.left_shift(jnp.int32(1), 31 - i)
        cand_s = cand ^ INT_MIN
        return jnp.where(count(lambda k: k >= cand_s) >= topk, cand, ans)

    ans = lax.fori_loop(0, 32, bit_step, jnp.zeros((tr, 1), I32))
    thr = ans ^ INT_MIN
    need = topk - count(lambda k: k > thr)

    r_i = lax.broadcasted_iota(I32, (LANES, LANES), 0)
    c_i = lax.broadcasted_iota(I32, (LANES, LANES), 1)
    upper = jnp.where(r_i < c_i, 1.0, 0.0).astype(BF16)

    def blk_step(j, carry):
        off = pl.multiple_of(j * LANES, LANES)
        kb = key_ref[:, pl.ds(off, LANES)]
        eq = jnp.where(kb == thr, jnp.where(kb == INT_MIN, 0.0, 1.0), 0.0)
        pre = _dot(eq.astype(BF16), upper) + carry
        take = jnp.where(pre < need, eq, 0.0)
        sel = jnp.where(kb > thr, 1.0, take)
        o_ref[:, pl.ds(off, LANES)] = jnp.where(sel > 0.5, 0.0, NEG_BIG)
        return carry + jnp.sum(eq, axis=-1, keepdims=True)

    def blk_step_all_ties(j, c):
        off = pl.multiple_of(j * LANES, LANES)
        kb = key_ref[:, pl.ds(off, LANES)]
        sel = jnp.where(kb >= thr, jnp.where(kb == INT_MIN, 0.0, 1.0), 0.0)
        o_ref[:, pl.ds(off, LANES)] = jnp.where(sel > 0.5, 0.0, NEG_BIG)
        return c

    n_eq = count(lambda k: jnp.logical_and(k == thr, k != INT_MIN))
    must_rank = jnp.max(jnp.where(n_eq > need, 1.0, 0.0)) > 0.5

    @pl.when(must_rank)
    def _():
        lax.fori_loop(0, nblk, blk_step, jnp.zeros((tr, 1), F32))

    @pl.when(jnp.logical_not(must_rank))
    def _():
        lax.fori_loop(0, nblk, blk_step_all_ties, 0)


def _select_kernel(sc_ref, qpos_ref, o_ref, key_ref, *, topk):
    _select_rows(sc_ref[...], qpos_ref[...], topk, key_ref, o_ref)


def _select(scores, qpos, topk, tr):
    R, S = scores.shape
    return pl.pallas_call(
        functools.partial(_select_kernel, topk=topk),
        grid=(R // tr,),
        in_specs=[pl.BlockSpec((tr, S), lambda i: (i, 0)),
                  pl.BlockSpec((tr, 1), lambda i: (i, 0))],
        out_specs=pl.BlockSpec((tr, S), lambda i: (i, 0)),
        out_shape=jax.ShapeDtypeStruct((R, S), F32),
        scratch_shapes=[pltpu.VMEM((tr, S), I32)],
        compiler_params=_cparams(("parallel",)),
    )(scores, qpos)


def _t5_bucket(dist):
    max_exact = T5_BUCKETS // 2
    is_small = dist < max_exact
    d = jnp.maximum(dist, 1).astype(F32)
    large = max_exact + (jnp.log(d / max_exact) / math.log(T5_MAX_DIST / max_exact)
                         * (T5_BUCKETS - max_exact)).astype(jnp.int32)
    large = jnp.minimum(large, T5_BUCKETS - 1)
    return jnp.where(is_small, dist, large)


def _bias_table(t5_bias, tq, tk):
    i = jnp.arange(tq, dtype=jnp.int32)[:, None]
    j = jnp.arange(tk, dtype=jnp.int32)[None, :]
    tabs = []
    buckets = jnp.arange(T5_BUCKETS, dtype=jnp.int32)
    for c in range(3):
        dist = jnp.maximum(c * tk + i - j, 0)
        onehot = (_t5_bucket(dist)[..., None] == buckets).astype(F32)
        tabs.append(jnp.einsum('ijb,bh->hij', onehot, t5_bias, precision=lax.Precision.HIGHEST))
    return jnp.stack(tabs).astype(F32).reshape(3, KV_HEADS, ATT_GROUP * tq, tk)


def _far_bucket_ok(tk):
    d = np.arange(tk + 1, 8 * tk + 2).astype(np.float32)
    me = T5_BUCKETS // 2
    large = me + (np.log(d / me) / math.log(T5_MAX_DIST / me) * (T5_BUCKETS - me)).astype(np.int32)
    return bool(np.all(np.minimum(large, T5_BUCKETS - 1) == T5_BUCKETS - 1))


Q_W = ATT_HEADS * ATT_DH
KV_W = KV_HEADS * ATT_DH
IQ_W = IDX_HEADS * IDX_DH
DSA_MAIN = Q_W + 2 * KV_W + IQ_W
ATT_SCALE = ATT_DH ** -0.5


def _dsa_project(x, w_in, kidx_g, kidx_b, tm):
    h_main = _matmul(x, w_in, 0, DSA_MAIN, tm, 1024)
    tail = w_in.shape[1] - DSA_MAIN
    w_tail = jnp.pad(w_in[:, DSA_MAIN:], ((0, 0), (0, 2 * LANES - tail)))
    ikw = _matmul(x, w_tail, 0, 2 * LANES, tm, 2 * LANES)
    ik = _ln_cols(ikw, 0, kidx_g, kidx_b, min(tm, 512))
    return h_main, ik, ikw


def _dsap_kernel(iq0_ref, iq1_ref, iw_ref, ik_ref, q_ref, k_ref, v_ref, tab_ref, o_ref,
                 iqs_ref, iwb_ref, sc_ref, key_ref, mb_ref, qs_ref, s_ref, m_ref, l_ref, acc_ref,
                 *, topk):
    qi = pl.program_id(1)
    tq = q_ref.shape[0]
    half = IDX_HEADS // 2
    nj = KEY_SB // LANES
    nsb = (qi + nj) // nj

    iw = iw_ref[...]
    for hh in range(IDX_HEADS):
        src = iq0_ref if hh < half else iq1_ref
        j = hh % half
        iqs_ref[hh * tq:(hh + 1) * tq, :] = src[:, j * IDX_DH:(j + 1) * IDX_DH].astype(BF16)
        iwb_ref[hh * tq:(hh + 1) * tq, :] = jnp.broadcast_to(
            _bf16_round(iw[:, hh:hh + 1] * IDX_SCALE), (tq, LANES))

    sc_ref[...] = jnp.zeros_like(sc_ref)

    def sc_sb(sb, c):
        for j in range(nj):
            off = pl.multiple_of(sb * KEY_SB + j * LANES, LANES)
            ikb = ik_ref[pl.ds(off, LANES), :].astype(BF16)
            r = _bf16_round(jnp.maximum(_dot_nt(iqs_ref[...], ikb), 0.0)) * iwb_ref[...]
            acc = r[0:tq]
            for hh in range(1, IDX_HEADS):
                acc = acc + r[hh * tq:(hh + 1) * tq]
            sc_ref[:, pl.ds(off, LANES)] = acc
        return c

    lax.fori_loop(0, nsb, sc_sb, 0)

    qpos = qi * tq + lax.broadcasted_iota(I32, (tq, 1), 0)
    _select_rows(sc_ref[...], qpos, topk, key_ref, mb_ref, nsb)

    for n in range(KV_HEADS):
        for g in range(ATT_GROUP):
            hh = n * ATT_GROUP + g
            qs_ref[g * tq:(g + 1) * tq, :] = q_ref[:, hh * ATT_DH:(hh + 1) * ATT_DH].astype(BF16)
        m_ref[...] = jnp.full_like(m_ref, -3e38)

        def stage_a(sb, c):
            off = pl.multiple_of(sb * KEY_SB, KEY_SB)
            kblk = k_ref[pl.ds(off, KEY_SB), n * ATT_DH:(n + 1) * ATT_DH].astype(BF16)
            s = _dot_nt(qs_ref[...], kblk) * ATT_SCALE
            m = m_ref[...]
            for j in range(nj):
                offj = pl.multiple_of(sb * KEY_SB + j * LANES, LANES)
                mbt = mb_ref[:, pl.ds(offj, LANES)]
                cls = jnp.clip(qi - (sb * nj + j), 0, 2)
                sj = (s[:, j * LANES:(j + 1) * LANES] + tab_ref[cls, n]
                      + jnp.concatenate([mbt] * ATT_GROUP, axis=0))
                s_ref[:, pl.ds(offj, LANES)] = sj
                m = jnp.maximum(m, sj)
            m_ref[...] = m
            return c

        lax.fori_loop(0, nsb, stage_a, 0)
        m_ref[...] = jnp.broadcast_to(jnp.max(m_ref[...], axis=-1, keepdims=True), m_ref.shape)
        l_ref[...] = jnp.zeros_like(l_ref)
        acc_ref[...] = jnp.zeros_like(acc_ref)

        def stage_b(sb, c):
            off = pl.multiple_of(sb * KEY_SB, KEY_SB)
            m = m_ref[...]
            l = l_ref[...]
            ps = []
            for j in range(nj):
                offj = pl.multiple_of(sb * KEY_SB + j * LANES, LANES)
                pj = jnp.exp(s_ref[:, pl.ds(offj, LANES)] - m)
                l = l + pj
                ps.append(pj.astype(BF16))
            l_ref[...] = l
            vblk = v_ref[pl.ds(off, KEY_SB), n * ATT_DH:(n + 1) * ATT_DH].astype(BF16)
            acc_ref[...] += _dot(jnp.concatenate(ps, axis=1), vblk)
            return c

        lax.fori_loop(0, nsb, stage_b, 0)
        o = acc_ref[...] / jnp.sum(l_ref[...], axis=-1, keepdims=True)
        for g in range(ATT_GROUP):
            hh = n * ATT_GROUP + g
            o_ref[:, hh * ATT_DH:(hh + 1) * ATT_DH] = o[g * tq:(g + 1) * tq].astype(o_ref.dtype)


def _dsa_prompt(x, w_in, kidx_g, kidx_b, t5_bias, B, T):
    M = B * T
    h_main, ik, ikw = _dsa_project(x, w_in, kidx_g, kidx_b, 1024 if M % 1024 == 0 else M)
    topk = min(TOPK_MAX, T // 4)
    tq = LANES
    assert _far_bucket_ok(LANES) and T % KEY_SB == 0
    nq = T // tq
    iqc = (Q_W + 2 * KV_W) // (IQ_W // 2)
    assert iqc * (IQ_W // 2) == Q_W + 2 * KV_W
    kcol = Q_W // KV_W
    gq = ATT_GROUP * tq
    o = pl.pallas_call(
        functools.partial(_dsap_kernel, topk=topk),
        grid=(B, nq),
        in_specs=[pl.BlockSpec((tq, IQ_W // 2), lambda b, qi: (b * nq + qi, iqc)),
                  pl.BlockSpec((tq, IQ_W // 2), lambda b, qi: (b * nq + qi, iqc + 1)),
                  pl.BlockSpec((tq, LANES), lambda b, qi: (b * nq + qi, 1)),
                  pl.BlockSpec((T, IDX_DH), lambda b, qi: (b, 0)),
                  pl.BlockSpec((tq, Q_W), lambda b, qi: (b * nq + qi, 0)),
                  pl.BlockSpec((T, KV_W), lambda b, qi: (b, kcol)),
                  pl.BlockSpec((T, KV_W), lambda b, qi: (b, kcol + 1)),
                  pl.BlockSpec((3, KV_HEADS, gq, LANES), lambda b, qi: (0, 0, 0, 0))],
        out_specs=pl.BlockSpec((tq, Q_W), lambda b, qi: (b * nq + qi, 0)),
        out_shape=jax.ShapeDtypeStruct((M, Q_W), BF16),
        scratch_shapes=[pltpu.VMEM((IDX_HEADS * tq, IDX_DH), BF16),
                        pltpu.VMEM((IDX_HEADS * tq, LANES), F32),
                        pltpu.VMEM((tq, T), F32),
                        pltpu.VMEM((tq, T), I32),
                        pltpu.VMEM((tq, T), F32),
                        pltpu.VMEM((gq, ATT_DH), BF16),
                        pltpu.VMEM((gq, T), F32),
                        pltpu.VMEM((gq, LANES), F32),
                        pltpu.VMEM((gq, LANES), F32),
                        pltpu.VMEM((gq, ATT_DH), F32)],
        compiler_params=_cparams(("parallel", "arbitrary")),
    )(h_main, h_main, ikw, ik, h_main, h_main, h_main, _bias_table(t5_bias, tq, LANES))
    k_new = h_main[:, Q_W:Q_W + KV_W]
    v_new = h_main[:, Q_W + KV_W:Q_W + 2 * KV_W]
    return o, k_new, v_new, ik


PAGES_PER_STEP = 32


def _head_rows(ref, n):
    return ref[0, pl.ds(n, LANES, stride=KV_HEADS), :]


def _idxs_kernel(pt_ref, iq_ref, iw_ref, *refs, n_steps):
    G = PAGES_PER_STEP
    page_refs, kn_ref, o_ref = refs[:G], refs[G], refs[G + 1]
    step = pl.program_id(1)
    tp = o_ref.shape[0]
    iq = iq_ref[0].astype(BF16)
    iwb = _bf16_round(iw_ref[0] * IDX_SCALE)

    def scores(keys):
        r = _bf16_round(jnp.maximum(_dot_nt(iq, keys.astype(BF16)), 0.0)) * iwb
        acc = r[0:tp]
        for hh in range(1, IDX_HEADS):
            acc = acc + r[hh * tp:(hh + 1) * tp]
        return acc

    @pl.when(step < n_steps - 1)
    def _():
        for j in range(G):
            o_ref[:, j * LANES:(j + 1) * LANES] = scores(page_refs[j][0])

    @pl.when(step == n_steps - 1)
    def _():
        o_ref[...] = jnp.zeros_like(o_ref)
        o_ref[:, 0:LANES] = scores(kn_ref[0])


def _attns_kernel(pt_ref, q_ref, *refs, n_steps, n_pages):
    G = PAGES_PER_STEP
    kp, vp = refs[:G], refs[G:2 * G]
    kn_ref, vn_ref, mb_ref, tab_ref, o_ref, m_ref, l_ref, acc_ref = refs[2 * G:]
    step = pl.program_id(1)

    @pl.when(step == 0)
    def _():
        m_ref[...] = jnp.full_like(m_ref, -3e38)
        l_ref[...] = jnp.zeros_like(l_ref)
        acc_ref[...] = jnp.zeros_like(acc_ref)

    def process(pages):
        for n in range(KV_HEADS):
            qn = q_ref[0, n].astype(BF16)
            s_parts = []
            for (kr, vr, cls, cb) in pages:
                mbt = mb_ref[:, cb * LANES:(cb + 1) * LANES]
                s_parts.append(_dot_nt(qn, _head_rows(kr, n).astype(BF16)) * ATT_SCALE
                               + tab_ref[cls, n] + jnp.concatenate([mbt] * ATT_GROUP, axis=0))
            s = s_parts[0] if len(s_parts) == 1 else jnp.concatenate(s_parts, axis=1)
            m_old = m_ref[n]
            m_new = jnp.maximum(m_old, jnp.max(s, axis=-1, keepdims=True))
            a = jnp.exp(m_old - m_new)
            p = jnp.exp(s - m_new)
            l_ref[n] = a * l_ref[n] + jnp.sum(p, axis=-1, keepdims=True)
            pv = None
            for j, (kr, vr, cls, cb) in enumerate(pages):
                d = _dot(p[:, j * LANES:(j + 1) * LANES].astype(BF16), _head_rows(vr, n).astype(BF16))
                pv = d if pv is None else pv + d
            acc_ref[n] = a * acc_ref[n] + pv
            m_ref[n] = m_new

    @pl.when(step < n_steps - 1)
    def _():
        process([(kp[j], vp[j], jnp.minimum(n_pages - (step * G + j), 2), j) for j in range(G)])

    @pl.when(step == n_steps - 1)
    def _():
        process([(kn_ref, vn_ref, 0, 0)])
        for n in range(KV_HEADS):
            o_ref[0, n] = acc_ref[n] / l_ref[n]


def _dsa_sample(x, cache_k, cache_v, cache_kidx, page_table, w_in, kidx_g, kidx_b, t5_bias,
                B, T, Tp, past_len):
    M = B * Tp
    G = PAGES_PER_STEP
    page = cache_kidx.shape[1]
    n_pages = past_len // page
    assert page == LANES and _far_bucket_ok(page) and n_pages % G == 0
    h_main, ik, ikw = _dsa_project(x, w_in, kidx_g, kidx_b, M)
    topk = min(TOPK_MAX, (past_len + T) // 4)
    n_steps = n_pages // G + 1
    S = n_steps * G * page
    n_pool = cache_kidx.shape[0]
    ck = cache_k.reshape(n_pool, page * KV_HEADS, ATT_DH)
    cv = cache_v.reshape(n_pool, page * KV_HEADS, ATT_DH)
    pt = page_table.reshape(-1).astype(jnp.int32)
    k_new = h_main[:, Q_W:Q_W + KV_W]
    v_new = h_main[:, Q_W + KV_W:Q_W + 2 * KV_W]

    def as_page(a, w):
        return jnp.pad(a.reshape(B, Tp, w), ((0, 0), (0, page - Tp), (0, 0))).reshape(B, -1, ATT_DH)

    iq = h_main[:, Q_W + 2 * KV_W:].reshape(B, Tp, IDX_HEADS, IDX_DH)
    iq_st = jnp.transpose(iq, (0, 2, 1, 3)).reshape(B, IDX_HEADS * Tp, IDX_DH)
    iw = ikw[:, LANES:LANES + IDX_HEADS].reshape(B, Tp, IDX_HEADS)
    iw_st = jnp.broadcast_to(jnp.transpose(iw, (0, 2, 1)).reshape(B, IDX_HEADS * Tp, 1),
                             (B, IDX_HEADS * Tp, LANES))

    def page_spec(rows, j):
        return pl.BlockSpec(
            (1, rows, ATT_DH),
            lambda b, s, p: (p[b * n_pages + jnp.minimum(s * G + j, n_pages - 1)], 0, 0))

    stacked = pl.BlockSpec((1, IDX_HEADS * Tp, LANES), lambda b, s, p: (b, 0, 0))
    scores = pl.pallas_call(
        functools.partial(_idxs_kernel, n_steps=n_steps),
        grid_spec=pltpu.PrefetchScalarGridSpec(
            num_scalar_prefetch=1,
            grid=(B, n_steps),
            in_specs=[stacked, stacked] + [page_spec(page, j) for j in range(G)]
                     + [pl.BlockSpec((1, page, IDX_DH), lambda b, s, p: (b, 0, 0))],
            out_specs=pl.BlockSpec((Tp, G * page), lambda b, s, p: (b, s)),
        ),
        out_shape=jax.ShapeDtypeStruct((M, S), F32),
        compiler_params=_cparams(("parallel", "arbitrary")),
    )(pt, iq_st, iw_st, *([cache_kidx] * G), as_page(ik, IDX_DH))
    qpos = jnp.tile(past_len + jnp.arange(Tp, dtype=jnp.int32), B)[:, None]
    mbias = _select(scores, qpos, topk, M)

    gq = ATT_GROUP * Tp
    q = h_main[:, :Q_W].reshape(B, Tp, KV_HEADS, ATT_GROUP, ATT_DH)
    q_st = jnp.transpose(q, (0, 2, 3, 1, 4)).reshape(B, KV_HEADS, gq, ATT_DH)
    qspec = pl.BlockSpec((1, KV_HEADS, gq, ATT_DH), lambda b, s, p: (b, 0, 0, 0))
    newspec = pl.BlockSpec((1, page * KV_HEADS, ATT_DH), lambda b, s, p: (b, 0, 0))
    o_st = pl.pallas_call(
        functools.partial(_attns_kernel, n_steps=n_steps, n_pages=n_pages),
        grid_spec=pltpu.PrefetchScalarGridSpec(
            num_scalar_prefetch=1,
            grid=(B, n_steps),
            in_specs=[qspec] + [page_spec(page * KV_HEADS, j) for j in range(G)] * 2
                     + [newspec, newspec,
                        pl.BlockSpec((Tp, G * page), lambda b, s, p: (b, s)),
                        pl.BlockSpec((3, KV_HEADS, gq, page), lambda b, s, p: (0, 0, 0, 0))],
            out_specs=qspec,
            scratch_shapes=[pltpu.VMEM((KV_HEADS, gq, 1), F32),
                            pltpu.VMEM((KV_HEADS, gq, 1), F32),
                            pltpu.VMEM((KV_HEADS, gq, ATT_DH), F32)],
        ),
        out_shape=jax.ShapeDtypeStruct((B, KV_HEADS, gq, ATT_DH), F32),
        compiler_params=_cparams(("parallel", "arbitrary")),
    )(pt, q_st, *([ck] * G), *([cv] * G), as_page(k_new, KV_W), as_page(v_new, KV_W), mbias,
      _bias_table(t5_bias, Tp, page))
    o = jnp.transpose(o_st.reshape(B, KV_HEADS, ATT_GROUP, Tp, ATT_DH), (0, 3, 1, 2, 4)).reshape(M, Q_W)
    return o, k_new, v_new, ik


def _router_kernel(x_ref, w_ref, b_ref, e_ref, g_ref, p_ref, cnt_ref, carry_ref):
    i = pl.program_id(0)
    tm = x_ref.shape[0]
    E = N_EXPERTS

    @pl.when(i == 0)
    def _():
        carry_ref[...] = jnp.zeros_like(carry_ref)

    logits = _dot(x_ref[...].astype(BF16), w_ref[...].astype(BF16)) + b_ref[...]
    lane = lax.broadcasted_iota(I32, (tm, E), 1).astype(F32)
    lane_o = lax.broadcasted_iota(I32, (tm, LANES), 1)
    vals, idxs = [], []
    cur = logits
    onehot = jnp.zeros((tm, E), F32)
    for _ in range(TOP_K):
        mx = jnp.max(cur, axis=-1, keepdims=True)
        ix = jnp.min(jnp.where(cur == mx, lane, float(E)), axis=-1, keepdims=True)
        hit = lane == ix
        onehot = jnp.where(hit, 1.0, onehot)
        cur = jnp.where(hit, -jnp.inf, cur)
        vals.append(mx)
        idxs.append(ix)
    ex = [jnp.exp(v - vals[0]) for v in vals]
    den = ex[0] + ex[1] + ex[2] + ex[3]
    r_i = lax.broadcasted_iota(I32, (tm, tm), 0)
    c_i = lax.broadcasted_iota(I32, (tm, tm), 1)
    lower = jnp.where(c_i < r_i, 1.0, 0.0).astype(BF16)
    prefix = _dot(lower, onehot.astype(BF16)) + carry_ref[...]
    e_out = jnp.zeros((tm, LANES), I32)
    g_out = jnp.zeros((tm, LANES), F32)
    p_out = jnp.zeros((tm, LANES), I32)
    for k in range(TOP_K):
        pos = jnp.sum(jnp.where(lane == idxs[k], prefix, 0.0), axis=-1, keepdims=True)
        e_out = jnp.where(lane_o == k, idxs[k].astype(I32), e_out)
        g_out = jnp.where(lane_o == k, ex[k] / den, g_out)
        p_out = jnp.where(lane_o == k, pos.astype(I32), p_out)
    e_ref[...] = e_out
    g_ref[...] = g_out
    p_ref[...] = p_out
    carry_ref[...] = carry_ref[...] + jnp.sum(onehot, axis=0, keepdims=True)
    cnt_ref[...] = carry_ref[...].astype(I32)


def _router(x, w_router, b_router, layer, tm):
    T, D = x.shape
    E = N_EXPERTS
    outs = pl.pallas_call(
        _router_kernel,
        grid=(T // tm,),
        in_specs=[pl.BlockSpec((tm, D), lambda i: (i, 0)),
                  pl.BlockSpec((None, D, E), lambda i: (layer, 0, 0)),
                  pl.BlockSpec((None, 1, E), lambda i: (layer, 0, 0))],
        out_specs=[pl.BlockSpec((tm, LANES), lambda i: (i, 0)),
                   pl.BlockSpec((tm, LANES), lambda i: (i, 0)),
                   pl.BlockSpec((tm, LANES), lambda i: (i, 0)),
                   pl.BlockSpec((1, E), lambda i: (0, 0))],
        out_shape=[jax.ShapeDtypeStruct((T, LANES), I32),
                   jax.ShapeDtypeStruct((T, LANES), F32),
                   jax.ShapeDtypeStruct((T, LANES), I32),
                   jax.ShapeDtypeStruct((1, E), I32)],
        scratch_shapes=[pltpu.VMEM((1, E), F32)],
        compiler_params=_cparams(("arbitrary",)),
    )(x, w_router, b_router.reshape(-1, 1, E))
    return outs


ISSUE_UNROLL = 8


def _row_copy(src, s_row, dst, d_row, sem):
    return pltpu.make_async_copy(src.at[pl.ds(s_row, 1)], dst.at[pl.ds(d_row, 1)], sem)


def _dispatch_kernel(nv_ref, vl_ref, tok_ref, x_hbm, o_ref, buf_ref, sem):
    i = pl.program_id(0)
    tm = o_ref.shape[0]
    n_valid = nv_ref[0]

    def issue(step, slot):
        base = vl_ref[step] * tm

        def body(r, c):
            _row_copy(x_hbm, tok_ref[base + r], buf_ref.at[slot], r, sem.at[slot]).start()
            return c

        lax.fori_loop(0, tm, body, 0, unroll=ISSUE_UNROLL)

    slot = i % 2

    @pl.when(jnp.logical_and(i == 0, n_valid > 0))
    def _():
        issue(0, 0)

    @pl.when(i + 1 < n_valid)
    def _():
        issue(i + 1, 1 - slot)

    @pl.when(i < n_valid)
    def _():
        pltpu.make_async_copy(x_hbm.at[pl.ds(0, tm)], buf_ref.at[slot], sem.at[slot]).wait()
        o_ref[...] = buf_ref[slot].astype(o_ref.dtype)

    @pl.when(i >= n_valid)
    def _():
        o_ref[...] = jnp.zeros_like(o_ref)


def _dispatch(x, row_tok, vlist, n_valid, tm):
    T, D = x.shape
    assert T >= tm
    n_rows = row_tok.shape[0]
    return pl.pallas_call(
        _dispatch_kernel,
        grid_spec=pltpu.PrefetchScalarGridSpec(
            num_scalar_prefetch=3,
            grid=(vlist.shape[0],),
            in_specs=[pl.BlockSpec(memory_space=pl.ANY)],
            out_specs=pl.BlockSpec((tm, D), lambda i, nv, vl, tok: (vl[i], 0)),
            scratch_shapes=[pltpu.VMEM((2, tm, D), x.dtype), pltpu.SemaphoreType.DMA((2,))],
        ),
        out_shape=jax.ShapeDtypeStruct((n_rows, D), BF16),
        compiler_params=_cparams(("arbitrary",)),
    )(n_valid, vlist, row_tok, x)


RUN_BLOCKS = 5
RUN_VARIANTS = (2, 4, 5)


def _for_run_variant(n, tm, body, o_ref):
    lo = 0
    for c in RUN_VARIANTS:
        @pl.when(jnp.logical_and(n > lo, n <= c))
        def _(c=c):
            body(c * tm)
            if c * tm < o_ref.shape[0]:
                o_ref[c * tm:, :] = jnp.zeros((o_ref.shape[0] - c * tm, o_ref.shape[1]), o_ref.dtype)
        lo = c

    @pl.when(n == 0)
    def _():
        o_ref[...] = jnp.zeros_like(o_ref)


def _gu_kernel(re_ref, rn_ref, rt_ref, x_ref, wg_ref, wu_ref, bg_ref, bu_ref, o_ref, *, tm):
    n = rn_ref[pl.program_id(1)]

    def body(m):
        xb = x_ref[0:m, :]
        gate = _dot(xb, wg_ref[...].astype(BF16)) + bg_ref[...]
        up = _dot(xb, wu_ref[...].astype(BF16)) + bu_ref[...]
        gate = jnp.minimum(gate, SWIGLU_LIMIT)
        up = jnp.clip(up, -SWIGLU_LIMIT, SWIGLU_LIMIT)
        sig = 1.0 / (1.0 + jnp.exp(-(gate * SWIGLU_ALPHA)))
        o_ref[0:m, :] = ((up + 1.0) * (gate * sig)).astype(o_ref.dtype)

    _for_run_variant(n, tm, body, o_ref)


def _down_kernel(re_ref, rn_ref, rt_ref, h_ref, w_ref, b_ref, o_ref, *, tm):
    n = rn_ref[pl.program_id(1)]

    def body(m):
        o_ref[0:m, :] = _dot(h_ref[0:m, :], w_ref[...].astype(BF16)) + b_ref[...]

    _for_run_variant(n, tm, body, o_ref)


def _experts(xs, run_e, run_n, run_total, w_gu, b_gu, w_down, b_down, layer, tm, tn_gu, tn_down):
    n_rows, D = xs.shape
    DE = w_down.shape[2]
    run = RUN_BLOCKS * tm
    NR = n_rows // run
    ng = DE // tn_gu
    rc = lambda r, rt: jnp.minimum(r, rt[0] - 1)
    hid = pl.pallas_call(
        functools.partial(_gu_kernel, tm=tm),
        grid_spec=pltpu.PrefetchScalarGridSpec(
            num_scalar_prefetch=3,
            grid=(ng, NR),
            in_specs=[pl.BlockSpec((run, D), lambda n, r, re, rn, rt: (rc(r, rt), 0)),
                      pl.BlockSpec((None, None, D, tn_gu), lambda n, r, re, rn, rt: (layer, re[r], 0, n)),
                      pl.BlockSpec((None, None, D, tn_gu), lambda n, r, re, rn, rt: (layer, re[r], 0, ng + n)),
                      pl.BlockSpec((None, None, 1, tn_gu), lambda n, r, re, rn, rt: (layer, re[r], 0, n)),
                      pl.BlockSpec((None, None, 1, tn_gu), lambda n, r, re, rn, rt: (layer, re[r], 0, ng + n))],
            out_specs=pl.BlockSpec((run, tn_gu), lambda n, r, re, rn, rt: (r, n)),
        ),
        out_shape=jax.ShapeDtypeStruct((n_rows, DE), BF16),
        compiler_params=_cparams(("arbitrary", "arbitrary")),
    )(run_e, run_n, run_total, xs, w_gu, w_gu, b_gu.reshape(DEPTH, N_EXPERTS, 1, 2 * DE),
      b_gu.reshape(DEPTH, N_EXPERTS, 1, 2 * DE))
    nd = D // tn_down
    out = pl.pallas_call(
        functools.partial(_down_kernel, tm=tm),
        grid_spec=pltpu.PrefetchScalarGridSpec(
            num_scalar_prefetch=3,
            grid=(nd, NR),
            in_specs=[pl.BlockSpec((run, DE), lambda n, r, re, rn, rt: (rc(r, rt), 0)),
                      pl.BlockSpec((None, None, DE, tn_down), lambda n, r, re, rn, rt: (layer, re[r], 0, n)),
                      pl.BlockSpec((None, None, 1, tn_down), lambda n, r, re, rn, rt: (layer, re[r], 0, n))],
            out_specs=pl.BlockSpec((run, tn_down), lambda n, r, re, rn, rt: (r, n)),
        ),
        out_shape=jax.ShapeDtypeStruct((n_rows, D), F32),
        compiler_params=_cparams(("arbitrary", "arbitrary")),
    )(run_e, run_n, run_total, hid, w_down, b_down.reshape(DEPTH, N_EXPERTS, 1, D))
    return out


def _combine_kernel(dest_ref, g_ref, x_ref, rows_hbm, lg_ref, lb_ref, o_ref, buf_ref, sem, *, tb):
    def issue(r, c):
        for k in range(TOP_K):
            _row_copy(rows_hbm, dest_ref[r * TOP_K + k], buf_ref.at[k], r, sem).start()
        return c

    lax.fori_loop(0, tb, issue, 0, unroll=ISSUE_UNROLL // TOP_K)
    for k in range(TOP_K):
        pltpu.make_async_copy(rows_hbm.at[pl.ds(0, tb)], buf_ref.at[k], sem).wait()
    g = _bf16_round(g_ref[...])
    y = _bf16_round(buf_ref[0]) * g[:, 0:1]
    for k in range(1, TOP_K):
        y = y + _bf16_round(buf_ref[k]) * g[:, k:k + 1]
    o_ref[...] = _ln_rows(DEEPNORM_ALPHA * x_ref[...] + y, lg_ref[...], lb_ref[...])


def _combine(x, rows, dest_flat, gates, ln_g, ln_b, tb):
    T, D = x.shape
    return pl.pallas_call(
        functools.partial(_combine_kernel, tb=tb),
        grid=(T // tb,),
        in_specs=[pl.BlockSpec((tb * TOP_K,), lambda i: (i,), memory_space=pltpu.SMEM),
                  pl.BlockSpec((tb, LANES), lambda i: (i, 0)),
                  pl.BlockSpec((tb, D), lambda i: (i, 0)),
                  pl.BlockSpec(memory_space=pl.ANY),
                  pl.BlockSpec((1, D), lambda i: (0, 0)),
                  pl.BlockSpec((1, D), lambda i: (0, 0))],
        out_specs=pl.BlockSpec((tb, D), lambda i: (i, 0)),
        out_shape=jax.ShapeDtypeStruct((T, D), F32),
        scratch_shapes=[pltpu.VMEM((TOP_K, tb, D), F32), pltpu.SemaphoreType.DMA(())],
        compiler_params=_cparams(("arbitrary",)),
    )(dest_flat, gates, x, rows, ln_g.reshape(1, D), ln_b.reshape(1, D))


def _moe_ln(xa, xb, w_router, b_router, w_gu, b_gu, w_down, b_down, ln_g, ln_b, layer, tm,
            tn_gu, tn_down):
    Ta, D = xa.shape
    Tb = xb.shape[0]
    E = N_EXPERTS
    tba = min(Ta, 256)
    ea, ga, pa, ca = _router(xa, w_router, b_router, layer, tba)
    eb, gb, pb, cb = _router(xb, w_router, b_router, layer, Tb)
    i32 = jnp.int32
    run = RUN_BLOCKS * tm
    ca, cb = ca[0], cb[0]
    counts = ca + cb
    runs_e = (counts + run - 1) // run
    run_end = jnp.cumsum(runs_e)
    run_start = run_end - runs_e
    total = run_end[-1]
    ea4, eb4 = ea[:, :TOP_K], eb[:, :TOP_K]

    def lookup(table, idx):
        hit = idx[..., None] == jnp.arange(E, dtype=i32)
        return jnp.sum(jnp.where(hit, table, 0), axis=-1)

    def count_le(sorted_vals, q):
        return jnp.sum((sorted_vals[None, :] <= q[:, None]).astype(i32), axis=1)

    dest_a = (lookup(run_start * run, ea4) + pa[:, :TOP_K]).reshape(-1).astype(i32)
    dest_b = (lookup(run_start * run + ca, eb4) + pb[:, :TOP_K]).reshape(-1).astype(i32)
    nr_max = E + -(-((Ta + Tb) * TOP_K) // run)
    r = jnp.arange(nr_max, dtype=i32)
    e_of_r = jnp.minimum(count_le(run_end, r), E - 1)
    rows_left = lookup(counts, e_of_r) - (r - lookup(run_start, e_of_r)) * run
    run_n = jnp.where(r < total, jnp.clip((rows_left + tm - 1) // tm, 0, RUN_BLOCKS), 0).astype(i32)
    last_e = jnp.sum(jnp.where(r == total - 1, e_of_r, 0))
    run_e = jnp.where(r < total, e_of_r, last_e).astype(i32)
    nblk = nr_max * RUN_BLOCKS
    b = jnp.arange(nblk, dtype=i32)
    valid = (b % RUN_BLOCKS) < jnp.repeat(run_n, RUN_BLOCKS)
    cum = jnp.cumsum(valid.astype(i32))
    n_valid = cum[-1]
    slot = jnp.where(valid, cum - 1, n_valid + b - cum)
    vlist = jnp.sum(jnp.where(slot[None, :] == b[:, None], b[None, :], 0), axis=1).astype(i32)
    tok = jnp.arange((Ta + Tb) * TOP_K, dtype=i32) // TOP_K
    row_tok = jnp.zeros((nblk * tm,), i32).at[jnp.concatenate([dest_a, dest_b])].set(tok)
    xs = _dispatch(jnp.concatenate([xa, xb], axis=0), row_tok, vlist, n_valid.reshape(1), tm)
    rows = _experts(xs, run_e, run_n, total.astype(i32).reshape(1), w_gu, b_gu, w_down, b_down,
                    layer, tm, tn_gu, tn_down)
    return (_combine(xa, rows, dest_a, ga, ln_g, ln_b, tba),
            _combine(xb, rows, dest_b, gb, ln_g, ln_b, Tb))


def kernel(x_prompt, x_sample, state_ret, cache_k, cache_v, cache_kidx, page_table, t5_bias, ret_w_in, ret_gn_g, ret_gn_b, ret_w_o, dsa_w_in, dsa_kidx_g, dsa_kidx_b, dsa_w_o, ln_mix_g, ln_mix_b, ln_ffn_g, ln_ffn_b, moe_w_router, moe_b_router, moe_w_gu, moe_b_gu, moe_w_down, moe_b_down):
    B, T, D = x_prompt.shape
    Bs, Ts, _ = x_sample.shape
    Tsp = SUBLANES
    past_len = page_table.shape[1] * cache_k.shape[2]
    Mp = B * T
    Ms = Bs * Tsp

    def pad_s(a):
        return jnp.pad(a.reshape(Bs, Ts, -1), ((0, 0), (0, Tsp - Ts), (0, 0))).reshape(Ms, -1)

    def unpad_s(a):
        return a.reshape(Bs, Tsp, -1)[:, :Ts].reshape(Bs * Ts, -1)

    xp = x_prompt.reshape(Mp, D)
    xs = x_sample.reshape(Bs * Ts, D)
    tm_p = 1024 if Mp % 1024 == 0 else Mp
    C = RET_CHUNK if T % RET_CHUNK == 0 else T

    moe = lambda xa, xb, i: _moe_ln(
        xa, xb, moe_w_router, moe_b_router, moe_w_gu, moe_b_gu, moe_w_down, moe_b_down,
        ln_ffn_g[i], ln_ffn_b[i], i, 256, 512, 1024)

    w_in, w_o = ret_w_in[0], ret_w_o[0]
    hp = _matmul(xp, w_in, 0, w_in.shape[1], tm_p, 1024)
    op, ret_p = _retention(hp, jnp.zeros((B,) + state_ret.shape[2:], F32), 0, B, T, C, C,
                           ret_gn_g[0], ret_gn_b[0], out_dtype=BF16)
    mp = _matmul(op, w_o, 0, D, tm_p, 512)
    xp = _res_ln(xp, mp, ln_mix_g[0], ln_mix_b[0], 256)

    xs_pad = pad_s(xs)
    hs = _matmul(xs_pad, w_in, 0, w_in.shape[1], Ms, 512)
    os_, ret_s = _retention(hs, state_ret[0], past_len, Bs, Ts, Ts, Tsp, ret_gn_g[0], ret_gn_b[0])
    ms = _matmul(os_, w_o, 0, D, Ms, 512)
    xs = unpad_s(_res_ln(xs_pad, ms, ln_mix_g[0], ln_mix_b[0], Ms))

    xp, xs = moe(xp, xs, 0)

    w_in, w_o = dsa_w_in[0], dsa_w_o[0]
    ap, k_p, v_p, ik_p = _dsa_prompt(xp, w_in, dsa_kidx_g[0], dsa_kidx_b[0], t5_bias, B, T)
    mp = _matmul(ap, w_o, 0, D, tm_p, 1024)
    xp = _res_ln(xp, mp, ln_mix_g[1], ln_mix_b[1], 256)

    xs_pad = pad_s(xs)
    as_, k_s, v_s, ik_s = _dsa_sample(xs_pad, cache_k[0], cache_v[0], cache_kidx[0], page_table, w_in,
                                      dsa_kidx_g[0], dsa_kidx_b[0], t5_bias, Bs, Ts, Tsp, past_len)
    ms = _matmul(as_, w_o, 0, D, Ms, 512)
    xs = unpad_s(_res_ln(xs_pad, ms, ln_mix_g[1], ln_mix_b[1], Ms))

    xp, xs = moe(xp, xs, 1)

    return (xp.reshape(B, T, D), xs.reshape(Bs, Ts, D),
            ret_p[None], ret_s[None],
            k_p.reshape(1, B, T, KV_HEADS, ATT_DH), v_p.reshape(1, B, T, KV_HEADS, ATT_DH),
            ik_p.reshape(1, B, T, IDX_DH),
            unpad_s(k_s).reshape(1, Bs, Ts, KV_HEADS, ATT_DH),
            unpad_s(v_s).reshape(1, Bs, Ts, KV_HEADS, ATT_DH),
            unpad_s(ik_s).reshape(1, Bs, Ts, IDX_DH))
```

```python
import functools
import math

import numpy as np
import jax
import jax.numpy as jnp
from jax import lax
from jax.experimental import pallas as pl
from jax.experimental.pallas import tpu as pltpu

F32 = jnp.float32
BF16 = jnp.bfloat16
I32 = jnp.int32

RET_HEADS = 8
RET_HB = 8
RET_CHUNK = 128
ROPE_BASE = 10000.0
ATT_HEADS = 16
ATT_DH = 128
KV_HEADS = 4
ATT_GROUP = ATT_HEADS // KV_HEADS
IDX_HEADS = 16
IDX_DH = 128
TOPK_MAX = 256
T5_BUCKETS = 32
T5_MAX_DIST = 128
N_EXPERTS = 32
TOP_K = 4
SWIGLU_LIMIT = 7.0
SWIGLU_ALPHA = 1.702
DEPTH = 2
DEEPNORM_ALPHA = (2.0 * DEPTH) ** 0.25
LN_EPS = 1e-5

LANES = 128
SUBLANES = 8
VMEM_LIMIT = 56 * 1024 * 1024
NEG_BIG = -1e30
INT_MIN = -(2 ** 31)


def _cparams(sem):
    return pltpu.CompilerParams(dimension_semantics=sem, vmem_limit_bytes=VMEM_LIMIT)


def _ln_rows(x, g, b):
    mu = jnp.mean(x, axis=-1, keepdims=True)
    xc = x - mu
    var = jnp.mean(xc * xc, axis=-1, keepdims=True)
    return xc * lax.rsqrt(var + LN_EPS) * g + b


def _bf16_round(a):
    return a.astype(BF16).astype(F32)


def _dot(a, b):
    return jnp.dot(a, b, preferred_element_type=F32)


def _dot_nt(a, b):
    return lax.dot_general(a, b, (((1,), (1,)), ((), ())), preferred_element_type=F32)


def _mm_kernel(x_ref, w_ref, o_ref, xb_ref):
    @pl.when(pl.program_id(1) == 0)
    def _():
        xb_ref[...] = x_ref[...].astype(BF16)

    o_ref[...] = _dot(xb_ref[...], w_ref[...].astype(BF16)).astype(o_ref.dtype)


def _mm_bf16_kernel(x_ref, w_ref, o_ref):
    o_ref[...] = _dot(x_ref[...], w_ref[...].astype(BF16)).astype(o_ref.dtype)


def _matmul(x, w, col0, ncols, tm, tn):
    M, K = x.shape
    assert M % tm == 0 and col0 % tn == 0 and ncols % tn == 0
    c0 = col0 // tn
    pre_rounded = x.dtype == BF16
    return pl.pallas_call(
        _mm_bf16_kernel if pre_rounded else _mm_kernel,
        grid=(M // tm, ncols // tn),
        in_specs=[pl.BlockSpec((tm, K), lambda i, j: (i, 0)),
                  pl.BlockSpec((K, tn), lambda i, j: (0, j + c0))],
        out_specs=pl.BlockSpec((tm, tn), lambda i, j: (i, j)),
        out_shape=jax.ShapeDtypeStruct((M, ncols), F32),
        scratch_shapes=[] if pre_rounded else [pltpu.VMEM((tm, K), BF16)],
        compiler_params=_cparams(("parallel", "arbitrary")),
    )(x, w)


def _res_ln_kernel(x_ref, h_ref, g_ref, b_ref, o_ref, *, alpha):
    o_ref[...] = _ln_rows(alpha * x_ref[...] + h_ref[...], g_ref[...], b_ref[...])


def _ln_kernel(x_ref, g_ref, b_ref, o_ref):
    o_ref[...] = _ln_rows(x_ref[...], g_ref[...], b_ref[...])


def _res_ln(x, h, g, b, tm):
    M, D = x.shape
    row = pl.BlockSpec((tm, D), lambda i: (i, 0))
    par = pl.BlockSpec((1, D), lambda i: (0, 0))
    return pl.pallas_call(
        functools.partial(_res_ln_kernel, alpha=DEEPNORM_ALPHA),
        grid=(M // tm,),
        in_specs=[row, row, par, par],
        out_specs=row,
        out_shape=jax.ShapeDtypeStruct((M, D), F32),
        compiler_params=_cparams(("parallel",)),
    )(x, h, g.reshape(1, D), b.reshape(1, D))


def _ln_cols(x, col_blk, g, b, tm):
    M = x.shape[0]
    D = LANES
    par = pl.BlockSpec((1, D), lambda i: (0, 0))
    return pl.pallas_call(
        _ln_kernel,
        grid=(M // tm,),
        in_specs=[pl.BlockSpec((tm, D), lambda i: (i, col_blk)), par, par],
        out_specs=pl.BlockSpec((tm, D), lambda i: (i, 0)),
        out_shape=jax.ShapeDtypeStruct((M, D), F32),
        compiler_params=_cparams(("parallel",)),
    )(x, g.reshape(1, D), b.reshape(1, D))


def _ret_kernel(q_ref, k_ref, v_ref, g_ref, cos_ref, sin_ref, intra_ref, qd_ref, kd_ref, sd_ref,
                s0_ref, gng_ref, gnb_ref, o_ref, so_ref, st_ref, *, dk):
    c = pl.program_id(2)
    half = dk // 2
    dv = v_ref.shape[1] // RET_HB

    @pl.when(c == 0)
    def _():
        st_ref[...] = s0_ref[0]

    cos = cos_ref[...]
    sin = sin_ref[...]

    def rot(x):
        x1, x2 = x[:, :half], x[:, half:]
        return jnp.concatenate([x1 * cos - x2 * sin, x1 * sin + x2 * cos], axis=-1)

    for j in range(RET_HB):
        q = rot(q_ref[:, j * dk:(j + 1) * dk])
        k = rot(k_ref[:, j * dk:(j + 1) * dk]) * (dk ** -0.5)
        vb = v_ref[:, j * dv:(j + 1) * dv].astype(BF16)
        qb = q.astype(BF16)
        st = st_ref[j]
        scores = _dot_nt(qb, k.astype(BF16)) * intra_ref[j]
        o = _dot(scores.astype(BF16), vb)
        o = o + _dot(qb, st.astype(BF16)) * qd_ref[j]
        kd = (k * kd_ref[j]).astype(BF16)
        new_st = st * sd_ref[j] + lax.dot_general(kd, vb, (((0,), (0,)), ((), ())),
                                                  preferred_element_type=F32)
        st_ref[j] = new_st
        so_ref[0, j] = new_st
        y = _ln_rows(o, gng_ref[j], gnb_ref[j])
        g = g_ref[:, j * dv:(j + 1) * dv]
        o_ref[:, j * dv:(j + 1) * dv] = (y * (g * (1.0 / (1.0 + jnp.exp(-g))))).astype(o_ref.dtype)


def _ret_log_decay():
    return jnp.log1p(-jnp.exp2(-5.0 - jnp.arange(RET_HEADS, dtype=F32)))


def _retention(h, state0, pos0, B, T, C, Cp, gn_g, gn_b, out_dtype=F32):
    H = RET_HEADS
    DK, DV = state0.shape[2], state0.shape[3]
    NC = T // C
    Tp = NC * Cp
    half = DK // 2
    log_g = _ret_log_decay()
    pos = (pos0 + jnp.arange(Tp, dtype=jnp.int32)).astype(F32)
    inv = ROPE_BASE ** (-jnp.arange(half, dtype=F32) / half)
    ang = pos[:, None] * inv[None, :]
    cos, sin = jnp.cos(ang), jnp.sin(ang)
    p = jnp.arange(C, dtype=F32)
    rel = p[:, None] - p[None, :]
    intra = jnp.where(rel[None] >= 0, jnp.exp(log_g[:, None, None] * jnp.maximum(rel, 0.0)[None]), 0.0)
    qd = jnp.exp((p[:, None] + 1.0) * log_g[None, :]).T[:, :, None]
    kd = jnp.exp((C - 1.0 - p)[:, None] * log_g[None, :]).T[:, :, None]
    sd = jnp.exp(C * log_g)[:, None, None]
    padc = Cp - C
    intra = jnp.pad(intra, ((0, 0), (0, padc), (0, padc)))
    qd = jnp.pad(qd, ((0, 0), (0, padc), (0, 0)))
    kd = jnp.pad(kd, ((0, 0), (0, padc), (0, 0)))

    HB = RET_HB
    assert H % HB == 0
    kb = H // HB
    vb = (2 * H * DK) // (HB * DV)
    gb = vb + H // HB
    assert vb * HB * DV == 2 * H * DK
    row = lambda b, hg, c: b * NC + c
    in_specs = [
        pl.BlockSpec((Cp, HB * DK), lambda b, hg, c: (row(b, hg, c), hg)),
        pl.BlockSpec((Cp, HB * DK), lambda b, hg, c: (row(b, hg, c), kb + hg)),
        pl.BlockSpec((Cp, HB * DV), lambda b, hg, c: (row(b, hg, c), vb + hg)),
        pl.BlockSpec((Cp, HB * DV), lambda b, hg, c: (row(b, hg, c), gb + hg)),
        pl.BlockSpec((Cp, half), lambda b, hg, c: (c, 0)),
        pl.BlockSpec((Cp, half), lambda b, hg, c: (c, 0)),
        pl.BlockSpec((HB, Cp, Cp), lambda b, hg, c: (hg, 0, 0)),
        pl.BlockSpec((HB, Cp, 1), lambda b, hg, c: (hg, 0, 0)),
        pl.BlockSpec((HB, Cp, 1), lambda b, hg, c: (hg, 0, 0)),
        pl.BlockSpec((HB, 1, 1), lambda b, hg, c: (hg, 0, 0)),
        pl.BlockSpec((1, HB, DK, DV), lambda b, hg, c: (b, hg, 0, 0)),
        pl.BlockSpec((HB, 1, DV), lambda b, hg, c: (hg, 0, 0)),
        pl.BlockSpec((HB, 1, DV), lambda b, hg, c: (hg, 0, 0)),
    ]
    out_specs = [
        pl.BlockSpec((Cp, HB * DV), lambda b, hg, c: (row(b, hg, c), hg)),
        pl.BlockSpec((1, HB, DK, DV), lambda b, hg, c: (b, hg, 0, 0)),
    ]
    o, st = pl.pallas_call(
        functools.partial(_ret_kernel, dk=DK),
        grid=(B, H // HB, NC),
        in_specs=in_specs,
        out_specs=out_specs,
        out_shape=[jax.ShapeDtypeStruct((B * Tp, H * DV), out_dtype),
                   jax.ShapeDtypeStruct((B, H, DK, DV), F32)],
        scratch_shapes=[pltpu.VMEM((HB, DK, DV), F32)],
        compiler_params=_cparams(("parallel", "parallel", "arbitrary")),
    )(h, h, h, h, cos, sin, intra, qd, kd, sd, state0,
      gn_g.reshape(H, 1, DV), gn_b.reshape(H, 1, DV))
    return o, st


IDX_SCALE = IDX_HEADS ** -0.5 * IDX_DH ** -0.5


KEY_SB = 4 * LANES


def _select_rows(sc, qpos, topk, key_ref, o_ref, nsb=None):
    tr, S = sc.shape
    col = lax.broadcasted_iota(I32, (tr, S), 1)
    valid = col <= qpos
    bits = pltpu.bitcast(sc, I32)
    key = jnp.where(sc == 0.0, 0, bits ^ ((bits >> 31) & 0x7FFFFFFF))
    key_ref[...] = jnp.where(valid, key, INT_MIN)

    if nsb is None:
        nblk = S // LANES

        def count(pred):
            return jnp.sum(jnp.where(pred(key_ref[...]), 1.0, 0.0), axis=-1, keepdims=True)
    else:
        nblk = nsb * (KEY_SB // LANES)

        def count(pred):
            def sb_step(sb, acc):
                off = pl.multiple_of(sb * KEY_SB, KEY_SB)
                w = jnp.where(pred(key_ref[:, pl.ds(off, KEY_SB)]), 1.0, 0.0)
                for j in range(KEY_SB // LANES):
                    acc = acc + w[:, j * LANES:(j + 1) * LANES]
                return acc

            acc = lax.fori_loop(0, nsb, sb_step, jnp.zeros((tr, LANES), F32))
            return jnp.sum(acc, axis=-1, keepdims=True)

    def bit_step(i, ans):
        cand = ans | jnp.left_shift(jnp.int32(1), 31 - i)
        cand_s = cand ^ INT_MIN
        return jnp.where(count(lambda k: k >= cand_s) >= topk, cand, ans)

    ans = lax.fori_loop(0, 32, bit_step, jnp.zeros((tr, 1), I32))
    thr = ans ^ INT_MIN
    need = topk - count(lambda k: k > thr)

    r_i = lax.broadcasted_iota(I32, (LANES, LANES), 0)
    c_i = lax.broadcasted_iota(I32, (LANES, LANES), 1)
    upper = jnp.where(r_i < c_i, 1.0, 0.0).astype(BF16)

    def blk_step(j, carry):
        off = pl.multiple_of(j * LANES, LANES)
        kb = key_ref[:, pl.ds(off, LANES)]
        eq = jnp.where(kb == thr, jnp.where(kb == INT_MIN, 0.0, 1.0), 0.0)
        pre = _dot(eq.astype(BF16), upper) + carry
        take = jnp.where(pre < need, eq, 0.0)
        sel = jnp.where(kb > thr, 1.0, take)
        o_ref[:, pl.ds(off, LANES)] = jnp.where(sel > 0.5, 0.0, NEG_BIG)
        return carry + jnp.sum(eq, axis=-1, keepdims=True)

    def blk_step_all_ties(j, c):
        off = pl.multiple_of(j * LANES, LANES)
        kb = key_ref[:, pl.ds(off, LANES)]
        sel = jnp.where(kb >= thr, jnp.where(kb == INT_MIN, 0.0, 1.0), 0.0)
        o_ref[:, pl.ds(off, LANES)] = jnp.where(sel > 0.5, 0.0, NEG_BIG)
        return c

    n_eq = count(lambda k: jnp.logical_and(k == thr, k != INT_MIN))
    must_rank = jnp.max(jnp.where(n_eq > need, 1.0, 0.0)) > 0.5

    @pl.when(must_rank)
    def _():
        lax.fori_loop(0, nblk, blk_step, jnp.zeros((tr, 1), F32))

    @pl.when(jnp.logical_not(must_rank))
    def _():
        lax.fori_loop(0, nblk, blk_step_all_ties, 0)


def _select_kernel(sc_ref, qpos_ref, o_ref, key_ref, *, topk):
    _select_rows(sc_ref[...], qpos_ref[...], topk, key_ref, o_ref)


def _select(scores, qpos, topk, tr):
    R, S = scores.shape
    return pl.pallas_call(
        functools.partial(_select_kernel, topk=topk),
        grid=(R // tr,),
        in_specs=[pl.BlockSpec((tr, S), lambda i: (i, 0)),
                  pl.BlockSpec((tr, 1), lambda i: (i, 0))],
        out_specs=pl.BlockSpec((tr, S), lambda i: (i, 0)),
        out_shape=jax.ShapeDtypeStruct((R, S), F32),
        scratch_shapes=[pltpu.VMEM((tr, S), I32)],
        compiler_params=_cparams(("parallel",)),
    )(scores, qpos)


def _t5_bucket(dist):
    max_exact = T5_BUCKETS // 2
    is_small = dist < max_exact
    d = jnp.maximum(dist, 1).astype(F32)
    large = max_exact + (jnp.log(d / max_exact) / math.log(T5_MAX_DIST / max_exact)
                         * (T5_BUCKETS - max_exact)).astype(jnp.int32)
    large = jnp.minimum(large, T5_BUCKETS - 1)
    return jnp.where(is_small, dist, large)


def _bias_table(t5_bias, tq, tk):
    i = jnp.arange(tq, dtype=jnp.int32)[:, None]
    j = jnp.arange(tk, dtype=jnp.int32)[None, :]
    tabs = []
    buckets = jnp.arange(T5_BUCKETS, dtype=jnp.int32)
    for c in range(3):
        dist = jnp.maximum(c * tk + i - j, 0)
        onehot = (_t5_bucket(dist)[..., None] == buckets).astype(F32)
        tabs.append(jnp.einsum('ijb,bh->hij', onehot, t5_bias, precision=lax.Precision.HIGHEST))
    return jnp.stack(tabs).astype(F32).reshape(3, KV_HEADS, ATT_GROUP * tq, tk)


def _far_bucket_ok(tk):
    d = np.arange(tk + 1, 8 * tk + 2).astype(np.float32)
    me = T5_BUCKETS // 2
    large = me + (np.log(d / me) / math.log(T5_MAX_DIST / me) * (T5_BUCKETS - me)).astype(np.int32)
    return bool(np.all(np.minimum(large, T5_BUCKETS - 1) == T5_BUCKETS - 1))


Q_W = ATT_HEADS * ATT_DH
KV_W = KV_HEADS * ATT_DH
IQ_W = IDX_HEADS * IDX_DH
DSA_MAIN = Q_W + 2 * KV_W + IQ_W
ATT_SCALE = ATT_DH ** -0.5


def _dsa_project(x, w_in, kidx_g, kidx_b, tm):
    h_main = _matmul(x, w_in, 0, DSA_MAIN, tm, 1024)
    tail = w_in.shape[1] - DSA_MAIN
    w_tail = jnp.pad(w_in[:, DSA_MAIN:], ((0, 0), (0, 2 * LANES - tail)))
    ikw = _matmul(x, w_tail, 0, 2 * LANES, tm, 2 * LANES)
    ik = _ln_cols(ikw, 0, kidx_g, kidx_b, min(tm, 512))
    return h_main, ik, ikw


def _dsap_kernel(iq0_ref, iq1_ref, iw_ref, ik_ref, q_ref, k_ref, v_ref, tab_ref, o_ref,
                 iqs_ref, iwb_ref, sc_ref, key_ref, mb_ref, qs_ref, s_ref, m_ref, l_ref, acc_ref,
                 *, topk):
    qi = pl.program_id(1)
    tq = q_ref.shape[0]
    half = IDX_HEADS // 2
    nj = KEY_SB // LANES
    nsb = (qi + nj) // nj

    iw = iw_ref[...]
    for hh in range(IDX_HEADS):
        src = iq0_ref if hh < half else iq1_ref
        j = hh % half
        iqs_ref[hh * tq:(hh + 1) * tq, :] = src[:, j * IDX_DH:(j + 1) * IDX_DH].astype(BF16)
        iwb_ref[hh * tq:(hh + 1) * tq, :] = jnp.broadcast_to(
            _bf16_round(iw[:, hh:hh + 1] * IDX_SCALE), (tq, LANES))

    sc_ref[...] = jnp.zeros_like(sc_ref)

    def sc_sb(sb, c):
        for j in range(nj):
            off = pl.multiple_of(sb * KEY_SB + j * LANES, LANES)
            ikb = ik_ref[pl.ds(off, LANES), :].astype(BF16)
            r = _bf16_round(jnp.maximum(_dot_nt(iqs_ref[...], ikb), 0.0)) * iwb_ref[...]
            acc = r[0:tq]
            for hh in range(1, IDX_HEADS):
                acc = acc + r[hh * tq:(hh + 1) * tq]
            sc_ref[:, pl.ds(off, LANES)] = acc
        return c

    lax.fori_loop(0, nsb, sc_sb, 0)

    qpos = qi * tq + lax.broadcasted_iota(I32, (tq, 1), 0)
    _select_rows(sc_ref[...], qpos, topk, key_ref, mb_ref, nsb)

    for n in range(KV_HEADS):
        for g in range(ATT_GROUP):
            hh = n * ATT_GROUP + g
            qs_ref[g * tq:(g + 1) * tq, :] = q_ref[:, hh * ATT_DH:(hh + 1) * ATT_DH].astype(BF16)
        m_ref[...] = jnp.full_like(m_ref, -3e38)

        def stage_a(sb, c):
            off = pl.multiple_of(sb * KEY_SB, KEY_SB)
            kblk = k_ref[pl.ds(off, KEY_SB), n * ATT_DH:(n + 1) * ATT_DH].astype(BF16)
            s = _dot_nt(qs_ref[...], kblk) * ATT_SCALE
            m = m_ref[...]
            for j in range(nj):
                offj = pl.multiple_of(sb * KEY_SB + j * LANES, LANES)
                mbt = mb_ref[:, pl.ds(offj, LANES)]
                cls = jnp.clip(qi - (sb * nj + j), 0, 2)
                sj = (s[:, j * LANES:(j + 1) * LANES] + tab_ref[cls, n]
                      + jnp.concatenate([mbt] * ATT_GROUP, axis=0))
                s_ref[:, pl.ds(offj, LANES)] = sj
                m = jnp.maximum(m, sj)
            m_ref[...] = m
            return c

        lax.fori_loop(0, nsb, stage_a, 0)
        m_ref[...] = jnp.broadcast_to(jnp.max(m_ref[...], axis=-1, keepdims=True), m_ref.shape)
        l_ref[...] = jnp.zeros_like(l_ref)
        acc_ref[...] = jnp.zeros_like(acc_ref)

        def stage_b(sb, c):
            off = pl.multiple_of(sb * KEY_SB, KEY_SB)
            m = m_ref[...]
            l = l_ref[...]
            ps = []
            for j in range(nj):
                offj = pl.multiple_of(sb * KEY_SB + j * LANES, LANES)
                pj = jnp.exp(s_ref[:, pl.ds(offj, LANES)] - m)
                l = l + pj
                ps.append(pj.astype(BF16))
            l_ref[...] = l
            vblk = v_ref[pl.ds(off, KEY_SB), n * ATT_DH:(n + 1) * ATT_DH].astype(BF16)
            acc_ref[...] += _dot(jnp.concatenate(ps, axis=1), vblk)
            return c

        lax.fori_loop(0, nsb, stage_b, 0)
        o = acc_ref[...] / jnp.sum(l_ref[...], axis=-1, keepdims=True)
        for g in range(ATT_GROUP):
            hh = n * ATT_GROUP + g
            o_ref[:, hh * ATT_DH:(hh + 1) * ATT_DH] = o[g * tq:(g + 1) * tq].astype(o_ref.dtype)


def _dsa_prompt(x, w_in, kidx_g, kidx_b, t5_bias, B, T):
    M = B * T
    h_main, ik, ikw = _dsa_project(x, w_in, kidx_g, kidx_b, 1024 if M % 1024 == 0 else M)
    topk = min(TOPK_MAX, T // 4)
    tq = LANES
    assert _far_bucket_ok(LANES) and T % KEY_SB == 0
    nq = T // tq
    iqc = (Q_W + 2 * KV_W) // (IQ_W // 2)
    assert iqc * (IQ_W // 2) == Q_W + 2 * KV_W
    kcol = Q_W // KV_W
    gq = ATT_GROUP * tq
    o = pl.pallas_call(
        functools.partial(_dsap_kernel, topk=topk),
        grid=(B, nq),
        in_specs=[pl.BlockSpec((tq, IQ_W // 2), lambda b, qi: (b * nq + qi, iqc)),
                  pl.BlockSpec((tq, IQ_W // 2), lambda b, qi: (b * nq + qi, iqc + 1)),
                  pl.BlockSpec((tq, LANES), lambda b, qi: (b * nq + qi, 1)),
                  pl.BlockSpec((T, IDX_DH), lambda b, qi: (b, 0)),
                  pl.BlockSpec((tq, Q_W), lambda b, qi: (b * nq + qi, 0)),
                  pl.BlockSpec((T, KV_W), lambda b, qi: (b, kcol)),
                  pl.BlockSpec((T, KV_W), lambda b, qi: (b, kcol + 1)),
                  pl.BlockSpec((3, KV_HEADS, gq, LANES), lambda b, qi: (0, 0, 0, 0))],
        out_specs=pl.BlockSpec((tq, Q_W), lambda b, qi: (b * nq + qi, 0)),
        out_shape=jax.ShapeDtypeStruct((M, Q_W), BF16),
        scratch_shapes=[pltpu.VMEM((IDX_HEADS * tq, IDX_DH), BF16),
                        pltpu.VMEM((IDX_HEADS * tq, LANES), F32),
                        pltpu.VMEM((tq, T), F32),
                        pltpu.VMEM((tq, T), I32),
                        pltpu.VMEM((tq, T), F32),
                        pltpu.VMEM((gq, ATT_DH), BF16),
                        pltpu.VMEM((gq, T), F32),
                        pltpu.VMEM((gq, LANES), F32),
                        pltpu.VMEM((gq, LANES), F32),
                        pltpu.VMEM((gq, ATT_DH), F32)],
        compiler_params=_cparams(("parallel", "arbitrary")),
    )(h_main, h_main, ikw, ik, h_main, h_main, h_main, _bias_table(t5_bias, tq, LANES))
    k_new = h_main[:, Q_W:Q_W + KV_W]
    v_new = h_main[:, Q_W + KV_W:Q_W + 2 * KV_W]
    return o, k_new, v_new, ik


PAGES_PER_STEP = 32


def _head_rows(ref, n):
    return ref[0, pl.ds(n, LANES, stride=KV_HEADS), :]


def _idxs_kernel(pt_ref, iq_ref, iw_ref, *refs, n_steps):
    G = PAGES_PER_STEP
    page_refs, kn_ref, o_ref = refs[:G], refs[G], refs[G + 1]
    step = pl.program_id(1)
    tp = o_ref.shape[0]
    iq = iq_ref[0].astype(BF16)
    iwb = _bf16_round(iw_ref[0] * IDX_SCALE)

    def scores(keys):
        r = _bf16_round(jnp.maximum(_dot_nt(iq, keys.astype(BF16)), 0.0)) * iwb
        acc = r[0:tp]
        for hh in range(1, IDX_HEADS):
            acc = acc + r[hh * tp:(hh + 1) * tp]
        return acc

    @pl.when(step < n_steps - 1)
    def _():
        for j in range(G):
            o_ref[:, j * LANES:(j + 1) * LANES] = scores(page_refs[j][0])

    @pl.when(step == n_steps - 1)
    def _():
        o_ref[...] = jnp.zeros_like(o_ref)
        o_ref[:, 0:LANES] = scores(kn_ref[0])


def _attns_kernel(pt_ref, q_ref, *refs, n_steps, n_pages):
    G = PAGES_PER_STEP
    kp, vp = refs[:G], refs[G:2 * G]
    kn_ref, vn_ref, mb_ref, tab_ref, o_ref, m_ref, l_ref, acc_ref = refs[2 * G:]
    step = pl.program_id(1)

    @pl.when(step == 0)
    def _():
        m_ref[...] = jnp.full_like(m_ref, -3e38)
        l_ref[...] = jnp.zeros_like(l_ref)
        acc_ref[...] = jnp.zeros_like(acc_ref)

    def process(pages):
        for n in range(KV_HEADS):
            qn = q_ref[0, n].astype(BF16)
            s_parts = []
            for (kr, vr, cls, cb) in pages:
                mbt = mb_ref[:, cb * LANES:(cb + 1) * LANES]
                s_parts.append(_dot_nt(qn, _head_rows(kr, n).astype(BF16)) * ATT_SCALE
                               + tab_ref[cls, n] + jnp.concatenate([mbt] * ATT_GROUP, axis=0))
            s = s_parts[0] if len(s_parts) == 1 else jnp.concatenate(s_parts, axis=1)
            m_old = m_ref[n]
            m_new = jnp.maximum(m_old, jnp.max(s, axis=-1, keepdims=True))
            a = jnp.exp(m_old - m_new)
            p = jnp.exp(s - m_new)
            l_ref[n] = a * l_ref[n] + jnp.sum(p, axis=-1, keepdims=True)
            pv = None
            for j, (kr, vr, cls, cb) in enumerate(pages):
                d = _dot(p[:, j * LANES:(j + 1) * LANES].astype(BF16), _head_rows(vr, n).astype(BF16))
                pv = d if pv is None else pv + d
            acc_ref[n] = a * acc_ref[n] + pv
            m_ref[n] = m_new

    @pl.when(step < n_steps - 1)
    def _():
        process([(kp[j], vp[j], jnp.minimum(n_pages - (step * G + j), 2), j) for j in range(G)])

    @pl.when(step == n_steps - 1)
    def _():
        process([(kn_ref, vn_ref, 0, 0)])
        for n in range(KV_HEADS):
            o_ref[0, n] = acc_ref[n] / l_ref[n]


def _dsa_sample(x, cache_k, cache_v, cache_kidx, page_table, w_in, kidx_g, kidx_b, t5_bias,
                B, T, Tp, past_len):
    M = B * Tp
    G = PAGES_PER_STEP
    page = cache_kidx.shape[1]
    n_pages = past_len // page
    assert page == LANES and _far_bucket_ok(page) and n_pages % G == 0
    h_main, ik, ikw = _dsa_project(x, w_in, kidx_g, kidx_b, M)
    topk = min(TOPK_MAX, (past_len + T) // 4)
    n_steps = n_pages // G + 1
    S = n_steps * G * page
    n_pool = cache_kidx.shape[0]
    ck = cache_k.reshape(n_pool, page * KV_HEADS, ATT_DH)
    cv = cache_v.reshape(n_pool, page * KV_HEADS, ATT_DH)
    pt = page_table.reshape(-1).astype(jnp.int32)
    k_new = h_main[:, Q_W:Q_W + KV_W]
    v_new = h_main[:, Q_W + KV_W:Q_W + 2 * KV_W]

    def as_page(a, w):
        return jnp.pad(a.reshape(B, Tp, w), ((0, 0), (0, page - Tp), (0, 0))).reshape(B, -1, ATT_DH)

    iq = h_main[:, Q_W + 2 * KV_W:].reshape(B, Tp, IDX_HEADS, IDX_DH)
    iq_st = jnp.transpose(iq, (0, 2, 1, 3)).reshape(B, IDX_HEADS * Tp, IDX_DH)
    iw = ikw[:, LANES:LANES + IDX_HEADS].reshape(B, Tp, IDX_HEADS)
    iw_st = jnp.broadcast_to(jnp.transpose(iw, (0, 2, 1)).reshape(B, IDX_HEADS * Tp, 1),
                             (B, IDX_HEADS * Tp, LANES))

    def page_spec(rows, j):
        return pl.BlockSpec(
            (1, rows, ATT_DH),
            lambda b, s, p: (p[b * n_pages + jnp.minimum(s * G + j, n_pages - 1)], 0, 0))

    stacked = pl.BlockSpec((1, IDX_HEADS * Tp, LANES), lambda b, s, p: (b, 0, 0))
    scores = pl.pallas_call(
        functools.partial(_idxs_kernel, n_steps=n_steps),
        grid_spec=pltpu.PrefetchScalarGridSpec(
            num_scalar_prefetch=1,
            grid=(B, n_steps),
            in_specs=[stacked, stacked] + [page_spec(page, j) for j in range(G)]
                     + [pl.BlockSpec((1, page, IDX_DH), lambda b, s, p: (b, 0, 0))],
            out_specs=pl.BlockSpec((Tp, G * page), lambda b, s, p: (b, s)),
        ),
        out_shape=jax.ShapeDtypeStruct((M, S), F32),
        compiler_params=_cparams(("parallel", "arbitrary")),
    )(pt, iq_st, iw_st, *([cache_kidx] * G), as_page(ik, IDX_DH))
    qpos = jnp.tile(past_len + jnp.arange(Tp, dtype=jnp.int32), B)[:, None]
    mbias = _select(scores, qpos, topk, M)

    gq = ATT_GROUP * Tp
    q = h_main[:, :Q_W].reshape(B, Tp, KV_HEADS, ATT_GROUP, ATT_DH)
    q_st = jnp.transpose(q, (0, 2, 3, 1, 4)).reshape(B, KV_HEADS, gq, ATT_DH)
    qspec = pl.BlockSpec((1, KV_HEADS, gq, ATT_DH), lambda b, s, p: (b, 0, 0, 0))
    newspec = pl.BlockSpec((1, page * KV_HEADS, ATT_DH), lambda b, s, p: (b, 0, 0))
    o_st = pl.pallas_call(
        functools.partial(_attns_kernel, n_steps=n_steps, n_pages=n_pages),
        grid_spec=pltpu.PrefetchScalarGridSpec(
            num_scalar_prefetch=1,
            grid=(B, n_steps),
            in_specs=[qspec] + [page_spec(page * KV_HEADS, j) for j in range(G)] * 2
                     + [newspec, newspec,
                        pl.BlockSpec((Tp, G * page), lambda b, s, p: (b, s)),
                        pl.BlockSpec((3, KV_HEADS, gq, page), lambda b, s, p: (0, 0, 0, 0))],
            out_specs=qspec,
            scratch_shapes=[pltpu.VMEM((KV_HEADS, gq, 1), F32),
                            pltpu.VMEM((KV_HEADS, gq, 1), F32),
                            pltpu.VMEM((KV_HEADS, gq, ATT_DH), F32)],
        ),
        out_shape=jax.ShapeDtypeStruct((B, KV_HEADS, gq, ATT_DH), F32),
        compiler_params=_cparams(("parallel", "arbitrary")),
    )(pt, q_st, *([ck] * G), *([cv] * G), as_page(k_new, KV_W), as_page(v_new, KV_W), mbias,
      _bias_table(t5_bias, Tp, page))
    o = jnp.transpose(o_st.reshape(B, KV_HEADS, ATT_GROUP, Tp, ATT_DH), (0, 3, 1, 2, 4)).reshape(M, Q_W)
    return o, k_new, v_new, ik


def _router_kernel(x_ref, w_ref, b_ref, e_ref, g_ref, p_ref, cnt_ref, carry_ref):
    i = pl.program_id(0)
    tm = x_ref.shape[0]
    E = N_EXPERTS

    @pl.when(i == 0)
    def _():
        carry_ref[...] = jnp.zeros_like(carry_ref)

    logits = _dot(x_ref[...].astype(BF16), w_ref[...].astype(BF16)) + b_ref[...]
    lane = lax.broadcasted_iota(I32, (tm, E), 1).astype(F32)
    lane_o = lax.broadcasted_iota(I32, (tm, LANES), 1)
    vals, idxs = [], []
    cur = logits
    onehot = jnp.zeros((tm, E), F32)
    for _ in range(TOP_K):
        mx = jnp.max(cur, axis=-1, keepdims=True)
        ix = jnp.min(jnp.where(cur == mx, lane, float(E)), axis=-1, keepdims=True)
        hit = lane == ix
        onehot = jnp.where(hit, 1.0, onehot)
        cur = jnp.where(hit, -jnp.inf, cur)
        vals.append(mx)
        idxs.append(ix)
    ex = [jnp.exp(v - vals[0]) for v in vals]
    den = ex[0] + ex[1] + ex[2] + ex[3]
    r_i = lax.broadcasted_iota(I32, (tm, tm), 0)
    c_i = lax.broadcasted_iota(I32, (tm, tm), 1)
    lower = jnp.where(c_i < r_i, 1.0, 0.0).astype(BF16)
    prefix = _dot(lower, onehot.astype(BF16)) + carry_ref[...]
    e_out = jnp.zeros((tm, LANES), I32)
    g_out = jnp.zeros((tm, LANES), F32)
    p_out = jnp.zeros((tm, LANES), I32)
    for k in range(TOP_K):
        pos = jnp.sum(jnp.where(lane == idxs[k], prefix, 0.0), axis=-1, keepdims=True)
        e_out = jnp.where(lane_o == k, idxs[k].astype(I32), e_out)
        g_out = jnp.where(lane_o == k, ex[k] / den, g_out)
        p_out = jnp.where(lane_o == k, pos.astype(I32), p_out)
    e_ref[...] = e_out
    g_ref[...] = g_out
    p_ref[...] = p_out
    carry_ref[...] = carry_ref[...] + jnp.sum(onehot, axis=0, keepdims=True)
    cnt_ref[...] = carry_ref[...].astype(I32)


def _router(x, w_router, b_router, layer, tm):
    T, D = x.shape
    E = N_EXPERTS
    outs = pl.pallas_call(
        _router_kernel,
        grid=(T // tm,),
        in_specs=[pl.BlockSpec((tm, D), lambda i: (i, 0)),
                  pl.BlockSpec((None, D, E), lambda i: (layer, 0, 0)),
                  pl.BlockSpec((None, 1, E), lambda i: (layer, 0, 0))],
        out_specs=[pl.BlockSpec((tm, LANES), lambda i: (i, 0)),
                   pl.BlockSpec((tm, LANES), lambda i: (i, 0)),
                   pl.BlockSpec((tm, LANES), lambda i: (i, 0)),
                   pl.BlockSpec((1, E), lambda i: (0, 0))],
        out_shape=[jax.ShapeDtypeStruct((T, LANES), I32),
                   jax.ShapeDtypeStruct((T, LANES), F32),
                   jax.ShapeDtypeStruct((T, LANES), I32),
                   jax.ShapeDtypeStruct((1, E), I32)],
        scratch_shapes=[pltpu.VMEM((1, E), F32)],
        compiler_params=_cparams(("arbitrary",)),
    )(x, w_router, b_router.reshape(-1, 1, E))
    return outs


ISSUE_UNROLL = 8


def _row_copy(src, s_row, dst, d_row, sem):
    return pltpu.make_async_copy(src.at[pl.ds(s_row, 1)], dst.at[pl.ds(d_row, 1)], sem)


def _dispatch_kernel(nv_ref, vl_ref, tok_ref, x_hbm, o_ref, buf_ref, sem):
    i = pl.program_id(0)
    tm = o_ref.shape[0]
    n_valid = nv_ref[0]

    def issue(step, slot):
        base = vl_ref[step] * tm

        def body(g, c):
            for u in range(ISSUE_UNROLL):
                r = g * ISSUE_UNROLL + u
                _row_copy(x_hbm, tok_ref[base + r], buf_ref.at[slot], r, sem.at[slot]).start(
                    priority=u % 2)
            return c

        lax.fori_loop(0, tm // ISSUE_UNROLL, body, 0)

    slot = i % 2

    @pl.when(jnp.logical_and(i == 0, n_valid > 0))
    def _():
        issue(0, 0)

    @pl.when(i + 1 < n_valid)
    def _():
        issue(i + 1, 1 - slot)

    @pl.when(i < n_valid)
    def _():
        pltpu.make_async_copy(x_hbm.at[pl.ds(0, tm)], buf_ref.at[slot], sem.at[slot]).wait()
        o_ref[...] = buf_ref[slot].astype(o_ref.dtype)

    @pl.when(i >= n_valid)
    def _():
        o_ref[...] = jnp.zeros_like(o_ref)


def _dispatch(x, row_tok, vlist, n_valid, tm):
    T, D = x.shape
    assert T >= tm
    n_rows = row_tok.shape[0]
    return pl.pallas_call(
        _dispatch_kernel,
        grid_spec=pltpu.PrefetchScalarGridSpec(
            num_scalar_prefetch=3,
            grid=(vlist.shape[0],),
            in_specs=[pl.BlockSpec(memory_space=pl.ANY)],
            out_specs=pl.BlockSpec((tm, D), lambda i, nv, vl, tok: (vl[i], 0)),
            scratch_shapes=[pltpu.VMEM((2, tm, D), x.dtype), pltpu.SemaphoreType.DMA((2,))],
        ),
        out_shape=jax.ShapeDtypeStruct((n_rows, D), BF16),
        compiler_params=_cparams(("arbitrary",)),
    )(n_valid, vlist, row_tok, x)


RUN_BLOCKS = 5
RUN_VARIANTS = (2, 4, 5)


def _for_run_variant(n, tm, body, o_ref):
    lo = 0
    for c in RUN_VARIANTS:
        @pl.when(jnp.logical_and(n > lo, n <= c))
        def _(c=c):
            body(c * tm)
            if c * tm < o_ref.shape[0]:
                o_ref[c * tm:, :] = jnp.zeros((o_ref.shape[0] - c * tm, o_ref.shape[1]), o_ref.dtype)
        lo = c

    @pl.when(n == 0)
    def _():
        o_ref[...] = jnp.zeros_like(o_ref)


def _gu_kernel(re_ref, rn_ref, rt_ref, x_ref, wg_ref, wu_ref, bg_ref, bu_ref, o_ref, *, tm):
    n = rn_ref[pl.program_id(1)]

    def body(m):
        xb = x_ref[0:m, :]
        gate = _dot(xb, wg_ref[...].astype(BF16)) + bg_ref[...]
        up = _dot(xb, wu_ref[...].astype(BF16)) + bu_ref[...]
        gate = jnp.minimum(gate, SWIGLU_LIMIT)
        up = jnp.clip(up, -SWIGLU_LIMIT, SWIGLU_LIMIT)
        sig = 1.0 / (1.0 + jnp.exp(-(gate * SWIGLU_ALPHA)))
        o_ref[0:m, :] = ((up + 1.0) * (gate * sig)).astype(o_ref.dtype)

    _for_run_variant(n, tm, body, o_ref)


def _down_kernel(re_ref, rn_ref, rt_ref, h_ref, w_ref, b_ref, o_ref, *, tm):
    n = rn_ref[pl.program_id(1)]

    def body(m):
        o_ref[0:m, :] = _dot(h_ref[0:m, :], w_ref[...].astype(BF16)) + b_ref[...]

    _for_run_variant(n, tm, body, o_ref)


def _experts(xs, run_e, run_n, run_total, w_gu, b_gu, w_down, b_down, layer, tm, tn_gu, tn_down):
    n_rows, D = xs.shape
    DE = w_down.shape[2]
    run = RUN_BLOCKS * tm
    NR = n_rows // run
    ng = DE // tn_gu
    rc = lambda r, rt: jnp.minimum(r, rt[0] - 1)
    hid = pl.pallas_call(
        functools.partial(_gu_kernel, tm=tm),
        grid_spec=pltpu.PrefetchScalarGridSpec(
            num_scalar_prefetch=3,
            grid=(ng, NR),
            in_specs=[pl.BlockSpec((run, D), lambda n, r, re, rn, rt: (rc(r, rt), 0)),
                      pl.BlockSpec((None, None, D, tn_gu), lambda n, r, re, rn, rt: (layer, re[r], 0, n)),
                      pl.BlockSpec((None, None, D, tn_gu), lambda n, r, re, rn, rt: (layer, re[r], 0, ng + n)),
                      pl.BlockSpec((None, None, 1, tn_gu), lambda n, r, re, rn, rt: (layer, re[r], 0, n)),
                      pl.BlockSpec((None, None, 1, tn_gu), lambda n, r, re, rn, rt: (layer, re[r], 0, ng + n))],
            out_specs=pl.BlockSpec((run, tn_gu), lambda n, r, re, rn, rt: (r, n)),
        ),
        out_shape=jax.ShapeDtypeStruct((n_rows, DE), BF16),
        compiler_params=_cparams(("arbitrary", "arbitrary")),
    )(run_e, run_n, run_total, xs, w_gu, w_gu, b_gu.reshape(DEPTH, N_EXPERTS, 1, 2 * DE),
      b_gu.reshape(DEPTH, N_EXPERTS, 1, 2 * DE))
    nd = D // tn_down
    out = pl.pallas_call(
        functools.partial(_down_kernel, tm=tm),
        grid_spec=pltpu.PrefetchScalarGridSpec(
            num_scalar_prefetch=3,
            grid=(nd, NR),
            in_specs=[pl.BlockSpec((run, DE), lambda n, r, re, rn, rt: (rc(r, rt), 0)),
                      pl.BlockSpec((None, None, DE, tn_down), lambda n, r, re, rn, rt: (layer, re[r], 0, n)),
                      pl.BlockSpec((None, None, 1, tn_down), lambda n, r, re, rn, rt: (layer, re[r], 0, n))],
            out_specs=pl.BlockSpec((run, tn_down), lambda n, r, re, rn, rt: (r, n)),
        ),
        out_shape=jax.ShapeDtypeStruct((n_rows, D), F32),
        compiler_params=_cparams(("arbitrary", "arbitrary")),
    )(run_e, run_n, run_total, hid, w_down, b_down.reshape(DEPTH, N_EXPERTS, 1, D))
    return out


def _combine_kernel(dest_ref, g_ref, x_ref, rows_hbm, lg_ref, lb_ref, o_ref, buf_ref, sem, *, tb):
    def issue(r, c):
        for k in range(TOP_K):
            _row_copy(rows_hbm, dest_ref[r * TOP_K + k], buf_ref.at[k], r, sem).start(priority=k % 2)
        return c

    lax.fori_loop(0, tb, issue, 0, unroll=ISSUE_UNROLL // TOP_K)
    for k in range(TOP_K):
        pltpu.make_async_copy(rows_hbm.at[pl.ds(0, tb)], buf_ref.at[k], sem).wait()
    g = _bf16_round(g_ref[...])
    y = _bf16_round(buf_ref[0]) * g[:, 0:1]
    for k in range(1, TOP_K):
        y = y + _bf16_round(buf_ref[k]) * g[:, k:k + 1]
    o_ref[...] = _ln_rows(DEEPNORM_ALPHA * x_ref[...] + y, lg_ref[...], lb_ref[...])


def _combine(x, rows, dest_flat, gates, ln_g, ln_b, tb):
    T, D = x.shape
    return pl.pallas_call(
        functools.partial(_combine_kernel, tb=tb),
        grid=(T // tb,),
        in_specs=[pl.BlockSpec((tb * TOP_K,), lambda i: (i,), memory_space=pltpu.SMEM),
                  pl.BlockSpec((tb, LANES), lambda i: (i, 0)),
                  pl.BlockSpec((tb, D), lambda i: (i, 0)),
                  pl.BlockSpec(memory_space=pl.ANY),
                  pl.BlockSpec((1, D), lambda i: (0, 0)),
                  pl.BlockSpec((1, D), lambda i: (0, 0))],
        out_specs=pl.BlockSpec((tb, D), lambda i: (i, 0)),
        out_shape=jax.ShapeDtypeStruct((T, D), F32),
        scratch_shapes=[pltpu.VMEM((TOP_K, tb, D), F32), pltpu.SemaphoreType.DMA(())],
        compiler_params=_cparams(("arbitrary",)),
    )(dest_flat, gates, x, rows, ln_g.reshape(1, D), ln_b.reshape(1, D))


def _moe_ln(xa, xb, w_router, b_router, w_gu, b_gu, w_down, b_down, ln_g, ln_b, layer, tm,
            tn_gu, tn_down):
    Ta, D = xa.shape
    Tb = xb.shape[0]
    E = N_EXPERTS
    tba = min(Ta, 256)
    ea, ga, pa, ca = _router(xa, w_router, b_router, layer, tba)
    eb, gb, pb, cb = _router(xb, w_router, b_router, layer, Tb)
    i32 = jnp.int32
    run = RUN_BLOCKS * tm
    ca, cb = ca[0], cb[0]
    counts = ca + cb
    runs_e = (counts + run - 1) // run
    run_end = jnp.cumsum(runs_e)
    run_start = run_end - runs_e
    total = run_end[-1]
    ea4, eb4 = ea[:, :TOP_K], eb[:, :TOP_K]

    def lookup(table, idx):
        hit = idx[..., None] == jnp.arange(E, dtype=i32)
        return jnp.sum(jnp.where(hit, table, 0), axis=-1)

    def count_le(sorted_vals, q):
        return jnp.sum((sorted_vals[None, :] <= q[:, None]).astype(i32), axis=1)

    dest_a = (lookup(run_start * run, ea4) + pa[:, :TOP_K]).reshape(-1).astype(i32)
    dest_b = (lookup(run_start * run + ca, eb4) + pb[:, :TOP_K]).reshape(-1).astype(i32)
    nr_max = E + -(-((Ta + Tb) * TOP_K) // run)
    r = jnp.arange(nr_max, dtype=i32)
    e_of_r = jnp.minimum(count_le(run_end, r), E - 1)
    rows_left = lookup(counts, e_of_r) - (r - lookup(run_start, e_of_r)) * run
    run_n = jnp.where(r < total, jnp.clip((rows_left + tm - 1) // tm, 0, RUN_BLOCKS), 0).astype(i32)
    last_e = jnp.sum(jnp.where(r == total - 1, e_of_r, 0))
    run_e = jnp.where(r < total, e_of_r, last_e).astype(i32)
    nblk = nr_max * RUN_BLOCKS
    b = jnp.arange(nblk, dtype=i32)
    valid = (b % RUN_BLOCKS) < jnp.repeat(run_n, RUN_BLOCKS)
    cum = jnp.cumsum(valid.astype(i32))
    n_valid = cum[-1]
    slot = jnp.where(valid, cum - 1, n_valid + b - cum)
    vlist = jnp.sum(jnp.where(slot[None, :] == b[:, None], b[None, :], 0), axis=1).astype(i32)
    tok = jnp.arange((Ta + Tb) * TOP_K, dtype=i32) // TOP_K
    row_tok = jnp.zeros((nblk * tm,), i32).at[jnp.concatenate([dest_a, dest_b])].set(tok)
    xs = _dispatch(jnp.concatenate([xa, xb], axis=0), row_tok, vlist, n_valid.reshape(1), tm)
    rows = _experts(xs, run_e, run_n, total.astype(i32).reshape(1), w_gu, b_gu, w_down, b_down,
                    layer, tm, tn_gu, tn_down)
    return (_combine(xa, rows, dest_a, ga, ln_g, ln_b, tba),
            _combine(xb, rows, dest_b, gb, ln_g, ln_b, Tb))


def kernel(x_prompt, x_sample, state_ret, cache_k, cache_v, cache_kidx, page_table, t5_bias, ret_w_in, ret_gn_g, ret_gn_b, ret_w_o, dsa_w_in, dsa_kidx_g, dsa_kidx_b, dsa_w_o, ln_mix_g, ln_mix_b, ln_ffn_g, ln_ffn_b, moe_w_router, moe_b_router, moe_w_gu, moe_b_gu, moe_w_down, moe_b_down):
    B, T, D = x_prompt.shape
    Bs, Ts, _ = x_sample.shape
    Tsp = SUBLANES
    past_len = page_table.shape[1] * cache_k.shape[2]
    Mp = B * T
    Ms = Bs * Tsp

    def pad_s(a):
        return jnp.pad(a.reshape(Bs, Ts, -1), ((0, 0), (0, Tsp - Ts), (0, 0))).reshape(Ms, -1)

    def unpad_s(a):
        return a.reshape(Bs, Tsp, -1)[:, :Ts].reshape(Bs * Ts, -1)

    xp = x_prompt.reshape(Mp, D)
    xs = x_sample.reshape(Bs * Ts, D)
    tm_p = 1024 if Mp % 1024 == 0 else Mp
    C = RET_CHUNK if T % RET_CHUNK == 0 else T

    moe = lambda xa, xb, i: _moe_ln(
        xa, xb, moe_w_router, moe_b_router, moe_w_gu, moe_b_gu, moe_w_down, moe_b_down,
        ln_ffn_g[i], ln_ffn_b[i], i, 256, 512, 1024)

    w_in, w_o = ret_w_in[0], ret_w_o[0]
    hp = _matmul(xp, w_in, 0, w_in.shape[1], tm_p, 1024)
    op, ret_p = _retention(hp, jnp.zeros((B,) + state_ret.shape[2:], F32), 0, B, T, C, C,
                           ret_gn_g[0], ret_gn_b[0], out_dtype=BF16)
    mp = _matmul(op, w_o, 0, D, tm_p, 512)
    xp = _res_ln(xp, mp, ln_mix_g[0], ln_mix_b[0], 256)

    xs_pad = pad_s(xs)
    hs = _matmul(xs_pad, w_in, 0, w_in.shape[1], Ms, 512)
    os_, ret_s = _retention(hs, state_ret[0], past_len, Bs, Ts, Ts, Tsp, ret_gn_g[0], ret_gn_b[0])
    ms = _matmul(os_, w_o, 0, D, Ms, 512)
    xs = unpad_s(_res_ln(xs_pad, ms, ln_mix_g[0], ln_mix_b[0], Ms))

    xp, xs = moe(xp, xs, 0)

    w_in, w_o = dsa_w_in[0], dsa_w_o[0]
    ap, k_p, v_p, ik_p = _dsa_prompt(xp, w_in, dsa_kidx_g[0], dsa_kidx_b[0], t5_bias, B, T)
    mp = _matmul(ap, w_o, 0, D, tm_p, 1024)
    xp = _res_ln(xp, mp, ln_mix_g[1], ln_mix_b[1], 256)

    xs_pad = pad_s(xs)
    as_, k_s, v_s, ik_s = _dsa_sample(xs_pad, cache_k[0], cache_v[0], cache_kidx[0], page_table, w_in,
                                      dsa_kidx_g[0], dsa_kidx_b[0], t5_bias, Bs, Ts, Tsp, past_len)
    ms = _matmul(as_, w_o, 0, D, Ms, 512)
    xs = unpad_s(_res_ln(xs_pad, ms, ln_mix_g[1], ln_mix_b[1], Ms))

    xp, xs = moe(xp, xs, 1)

    return (xp.reshape(B, T, D), xs.reshape(Bs, Ts, D),
            ret_p[None], ret_s[None],
            k_p.reshape(1, B, T, KV_HEADS, ATT_DH), v_p.reshape(1, B, T, KV_HEADS, ATT_DH),
            ik_p.reshape(1, B, T, IDX_DH),
            unpad_s(k_s).reshape(1, Bs, Ts, KV_HEADS, ATT_DH),
            unpad_s(v_s).reshape(1, Bs, Ts, KV_HEADS, ATT_DH),
            unpad_s(ik_s).reshape(1, Bs, Ts, IDX_DH))
```
